```python
import math
import jax, jax.numpy as jnp
from jax import lax
import numpy as np

D_MODEL = 1024
BATCH = 8
SEQ = 2048
DEPTH = 4

GRID_W = 64
CTX_LEN = 256
EPS = 1e-6
N_MOD = 6
DA_HEADS = 4
DA_HEAD_DIM = 64
DA_WIDTH = DA_HEADS * 2 * DA_HEAD_DIM
ROPE_BASE = 10000.0
Q_BLOCK = 128
NA_HEADS = 8
NA_HEAD_DIM = 64
NA_WIDTH = NA_HEADS * NA_HEAD_DIM
NA_KH_MAX = 8
NA_KW = 16
FT_GROUPS = 4
FT_GROUP_DIM = 128
FT_WIDTH = FT_GROUPS * FT_GROUP_DIM
BRANCH_WIDTH = 512
N_BRANCHES = 3
IN_SPLITS = [DA_WIDTH, DA_WIDTH, DA_WIDTH, NA_WIDTH, NA_WIDTH, NA_WIDTH, FT_WIDTH, N_BRANCHES * D_MODEL]
IN_COLS = sum(IN_SPLITS)
N_EXPERTS = 32
TOP_K = 4
D_FF = D_MODEL
SWIGLU_LIMIT = 7.0
SWIGLU_ALPHA = 1.702
MOE_BLOCK = 256

kernel_name = "hybrid_diffusion_parallel_mixer_moe"


def rmsnorm(x, g):
    xf = x.astype(jnp.float32)
    xf = xf * lax.rsqrt(jnp.mean(xf * xf, axis=-1, keepdims=True) + EPS)
    return (xf * g.astype(jnp.float32)).astype(x.dtype)


def axial_rope_tables(n):
    t = jnp.arange(n)
    row = (t // GRID_W).astype(jnp.float32)
    col = (t % GRID_W).astype(jnp.float32)
    quarter = DA_HEAD_DIM // 4
    freqs = ROPE_BASE ** (-jnp.arange(quarter, dtype=jnp.float32) / quarter)
    ang = jnp.stack([row[:, None] * freqs, col[:, None] * freqs], axis=1)
    return jnp.cos(ang), jnp.sin(ang)


def apply_axial_rope(x, cos, sin):
    xs = x.astype(jnp.float32).reshape(x.shape[:-1] + (2, 2, x.shape[-1] // 4))
    x1, x2 = xs[..., 0, :], xs[..., 1, :]
    c = cos[None, :, None, None]
    s = sin[None, :, None, None]
    out = jnp.stack([x1 * c - x2 * s, x1 * s + x2 * c], axis=-2)
    return out.reshape(x.shape).astype(x.dtype)


def diff_attend(q, k, v, lam, subln_g, lam_init):
    s = jnp.einsum('bqhmd,bkhmd->bhmqk', q, k).astype(jnp.float32) * (DA_HEAD_DIM ** -0.5)
    p = jax.nn.softmax(s, axis=-1)
    a = p[:, :, 0] - lam * p[:, :, 1]
    o = jnp.einsum('bhqk,bkhe->bqhe', a.astype(v.dtype), v)
    o = rmsnorm(o, subln_g) * (1.0 - lam_init)
    return o.reshape(o.shape[0], o.shape[1], DA_WIDTH)


def diff_attention_latent(q, k, v, lam, subln_g, lam_init):
    B, S = q.shape[0], q.shape[1]
    nb = S // Q_BLOCK
    qb = q.reshape(B, nb, Q_BLOCK, DA_HEADS, 2, DA_HEAD_DIM).swapaxes(0, 1)
    ob = lax.map(lambda qi: diff_attend(qi, k, v, lam, subln_g, lam_init), qb)
    return ob.swapaxes(0, 1).reshape(B, S, DA_WIDTH)


def neighbourhood_attention(q, k, v, k_ctx, v_ctx, rpb):
    B, S, H, d = q.shape
    rows = S // GRID_W
    kh = min(NA_KH_MAX, rows)
    r = jnp.arange(rows)
    row_start = jnp.clip(r - kh // 2, 0, rows - kh)
    key_rows = row_start[:, None] + jnp.arange(kh)[None, :]
    col = jnp.arange(GRID_W)
    col_start = jnp.clip(col - NA_KW // 2, 0, GRID_W - NA_KW)
    col_mask = (col[None, :] >= col_start[:, None]) & (col[None, :] < col_start[:, None] + NA_KW)
    qg = q.reshape(B, rows, GRID_W, H, d)
    kg = k.reshape(B, rows, GRID_W, H, d)[:, key_rows]
    vg = v.reshape(B, rows, GRID_W, H, d)[:, key_rows]
    scale = d ** -0.5
    s_nb = jnp.einsum('brqhd,brjkhd->brhqjk', qg, kg).astype(jnp.float32) * scale
    dr = key_rows - r[:, None] + NA_KH_MAX - 1
    dc = jnp.clip(col[None, :] - col[:, None], 1 - NA_KW, NA_KW - 1) + NA_KW - 1
    bias = rpb.astype(jnp.float32)[:, dr[:, :, None, None], dc[None, None]]
    s_nb = s_nb + bias.transpose(1, 0, 3, 2, 4)[None]
    s_nb = jnp.where(col_mask[:, None, :], s_nb, -jnp.inf)
    s_ctx = jnp.einsum('brqhd,bkhd->brhqk', qg, k_ctx).astype(jnp.float32) * scale
    n_nb = kh * GRID_W
    s_all = jnp.concatenate([s_nb.reshape(B, rows, H, GRID_W, n_nb), s_ctx], axis=-1)
    p = jax.nn.softmax(s_all, axis=-1).astype(v.dtype)
    p_nb = p[..., :n_nb].reshape(B, rows, H, GRID_W, kh, GRID_W)
    p_ctx = p[..., n_nb:]
    o = (jnp.einsum('brhqjk,brjkhd->brqhd', p_nb, vg)
         + jnp.einsum('brhqk,bkhd->brqhd', p_ctx, v_ctx))
    return o.reshape(B, S, H * d)


def context_attention(q, k, v):
    s = jnp.einsum('bqhd,bkhd->bhqk', q, k).astype(jnp.float32) * (q.shape[-1] ** -0.5)
    p = jax.nn.softmax(s, axis=-1).astype(v.dtype)
    o = jnp.einsum('bhqk,bkhd->bqhd', p, v)
    return o.reshape(q.shape[0], q.shape[1], -1)


def fourier_mix(u):
    B, n, _ = u.shape
    ug = u.astype(jnp.float32).reshape(B, n, FT_GROUPS, FT_GROUP_DIM)
    f = jnp.fft.fftn(ug, axes=(1, 3), norm='ortho').real
    return f.reshape(B, n, FT_WIDTH).astype(u.dtype)


def merge_branches(oa, ob, oc, gate_logits, w_branch_l, w_out_l):
    o = jnp.stack([oa, ob, oc], axis=2)
    proj = jnp.einsum('bnie,ied->bnid', o, w_branch_l)
    g = jax.nn.sigmoid(gate_logits.reshape(gate_logits.shape[:-1] + (N_BRANCHES, D_MODEL)))
    return jnp.sum(g * proj, axis=2) @ w_out_l


def moe_ffn(h, w_router, b_router, w_gate_up, b_gate_up, w_down, b_down):
    lead = h.shape[:-1]
    hf = h.reshape(-1, D_MODEL)
    T = hf.shape[0]
    logits = (hf @ w_router + b_router).astype(jnp.float32)
    top_val, top_idx = lax.top_k(logits, TOP_K)
    top_w = jax.nn.softmax(top_val, axis=-1).astype(h.dtype)
    n_assign = T * TOP_K
    flat_e = top_idx.reshape(-1)
    order = jnp.argsort(flat_e)
    sorted_e = flat_e[order]
    sorted_tok = (order // TOP_K).astype(jnp.int32)
    sorted_w = top_w.reshape(-1)[order]
    counts = jnp.bincount(flat_e, length=N_EXPERTS)
    padded = (counts + MOE_BLOCK - 1) // MOE_BLOCK * MOE_BLOCK
    pad_end = jnp.cumsum(padded)
    pad_start = pad_end - padded
    grp_start = jnp.cumsum(counts) - counts
    slot = pad_start[sorted_e] + jnp.arange(n_assign) - grp_start[sorted_e]
    n_blocks = -(-n_assign // MOE_BLOCK) + N_EXPERTS
    n_slots = n_blocks * MOE_BLOCK
    slot_tok = jnp.full((n_slots,), T, jnp.int32).at[slot].set(sorted_tok)
    slot_w = jnp.zeros((n_slots,), h.dtype).at[slot].set(sorted_w)
    block_e = jnp.minimum(jnp.searchsorted(pad_end, jnp.arange(n_blocks) * MOE_BLOCK, side='right'),
                          N_EXPERTS - 1)
    h_pad = jnp.concatenate([hf, jnp.zeros((1, D_MODEL), h.dtype)], axis=0)
    xb = h_pad[slot_tok].reshape(n_blocks, MOE_BLOCK, D_MODEL)

    def expert_block(args):
        xe, e = args
        gu = xe @ w_gate_up[e] + b_gate_up[e]
        gate = jnp.minimum(gu[:, :D_FF], SWIGLU_LIMIT)
        up = jnp.clip(gu[:, D_FF:], -SWIGLU_LIMIT, SWIGLU_LIMIT)
        act = gate * jax.nn.sigmoid(SWIGLU_ALPHA * gate) * (up + 1.0)
        return act @ w_down[e] + b_down[e]

    yb = lax.map(expert_block, (xb, block_e)).reshape(n_slots, D_MODEL)
    out = jnp.zeros((T + 1, D_MODEL), h.dtype).at[slot_tok].add(yb * slot_w[:, None])
    return out[:T].reshape(lead + (D_MODEL,))


def setup_inputs(seed: int = 0) -> dict:
    key = jax.random.key(seed)
    ks = jax.random.split(key, 24)
    f32 = jnp.float32
    nrm = lambda k, shape, s: (jax.random.normal(k, shape, f32) * s)
    D = D_MODEL
    return {
        'x': nrm(ks[0], (BATCH, SEQ, D), 1.0),
        'c': nrm(ks[1], (BATCH, D), 1.0),
        'ctx': nrm(ks[2], (BATCH, CTX_LEN, D), 1.0),
        'c_ctx': nrm(ks[3], (D,), 1.0),
        'w_mod': nrm(ks[4], (DEPTH, D, N_MOD * D), 0.5 * D ** -0.5),
        'b_mod': nrm(ks[5], (DEPTH, N_MOD * D), 0.02),
        'norm_mix_g': 1.0 + nrm(ks[6], (DEPTH, D), 0.02),
        'norm_ffn_g': 1.0 + nrm(ks[7], (DEPTH, D), 0.02),
        'w_in': nrm(ks[8], (DEPTH, D, IN_COLS), D ** -0.5),
        'da_lambda': nrm(ks[9], (DEPTH, 4, DA_HEAD_DIM), 0.1),
        'da_subln_g': 1.0 + nrm(ks[10], (DEPTH, 2 * DA_HEAD_DIM), 0.02),
        'na_rpb': nrm(ks[11], (DEPTH, NA_HEADS, 2 * NA_KH_MAX - 1, 2 * NA_KW - 1), 0.1),
        'w_branch': nrm(ks[12], (DEPTH, N_BRANCHES, BRANCH_WIDTH, D), BRANCH_WIDTH ** -0.5),
        'w_out': nrm(ks[13], (DEPTH, D, D), D ** -0.5),
        'w_router': nrm(ks[14], (DEPTH, D, N_EXPERTS), D ** -0.5),
        'b_router': nrm(ks[15], (DEPTH, N_EXPERTS), 0.01),
        'w_gate_up': nrm(ks[16], (DEPTH, N_EXPERTS, D, 2 * D_FF), D ** -0.5),
        'b_gate_up': nrm(ks[17], (DEPTH, N_EXPERTS, 2 * D_FF), 0.02),
        'w_down': nrm(ks[18], (DEPTH, N_EXPERTS, D_FF, D), D_FF ** -0.5),
        'b_down': nrm(ks[19], (DEPTH, N_EXPERTS, D), 0.02),
        'final_g': 1.0 + nrm(ks[20], (D,), 0.02),
    }


def reference(x, c, ctx, c_ctx, w_mod, b_mod, norm_mix_g, norm_ffn_g, w_in, da_lambda,
              da_subln_g, na_rpb, w_branch, w_out, w_router, b_router, w_gate_up,
              b_gate_up, w_down, b_down, final_g):
    B, S, _ = x.shape
    C = ctx.shape[1]
    split_pts = list(np.cumsum(IN_SPLITS)[:-1])
    cos, sin = axial_rope_tables(S)
    silu_c = jax.nn.silu(c)
    silu_cc = jax.nn.silu(c_ctx)[None]
    y = ctx
    for l in range(DEPTH):
        last = l == DEPTH - 1
        lam_init = 0.8 - 0.6 * math.exp(-0.3 * l)
        lq1, lk1, lq2, lk2 = [da_lambda[l, i].astype(jnp.float32) for i in range(4)]
        lam = jnp.exp(jnp.sum(lq1 * lk1)) - jnp.exp(jnp.sum(lq2 * lk2)) + lam_init
        sh1x, sc1x, g1x, sh2x, sc2x, g2x = [m[:, None] for m in jnp.split(silu_c @ w_mod[l] + b_mod[l], N_MOD, axis=-1)]
        sh1y, sc1y, g1y, sh2y, sc2y, g2y = [m[:, None] for m in jnp.split(silu_cc @ w_mod[l] + b_mod[l], N_MOD, axis=-1)]

        hx = rmsnorm(x, norm_mix_g[l]) * (1.0 + sc1x) + sh1x
        hy = rmsnorm(y, norm_mix_g[l]) * (1.0 + sc1y) + sh1y
        qa_x, ka_x, va_x, qn_x, kn_x, vn_x, f_x, gt_x = jnp.split(hx @ w_in[l], split_pts, axis=-1)
        qa_y, ka_y, va_y, qn_y, kn_y, vn_y, f_y, gt_y = jnp.split(hy @ w_in[l], split_pts, axis=-1)

        qa_x = apply_axial_rope(qa_x.reshape(B, S, DA_HEADS, 2, DA_HEAD_DIM), cos, sin)
        ka_x = apply_axial_rope(ka_x.reshape(B, S, DA_HEADS, 2, DA_HEAD_DIM), cos, sin)
        ka_y = ka_y.reshape(B, C, DA_HEADS, 2, DA_HEAD_DIM)
        va_x = va_x.reshape(B, S, DA_HEADS, 2 * DA_HEAD_DIM)
        va_y = va_y.reshape(B, C, DA_HEADS, 2 * DA_HEAD_DIM)
        k_cat = jnp.concatenate([ka_x, ka_y], axis=1)
        v_cat = jnp.concatenate([va_x, va_y], axis=1)
        oa_x = diff_attention_latent(qa_x, k_cat, v_cat, lam, da_subln_g[l], lam_init)

        kn_y = kn_y.reshape(B, C, NA_HEADS, NA_HEAD_DIM)
        vn_y = vn_y.reshape(B, C, NA_HEADS, NA_HEAD_DIM)
        ob_x = neighbourhood_attention(qn_x.reshape(B, S, NA_HEADS, NA_HEAD_DIM),
                                       kn_x.reshape(B, S, NA_HEADS, NA_HEAD_DIM),
                                       vn_x.reshape(B, S, NA_HEADS, NA_HEAD_DIM),
                                       kn_y, vn_y, na_rpb[l])

        oc_x = fourier_mix(f_x)
        mix_x = merge_branches(oa_x, ob_x, oc_x, gt_x, w_branch[l], w_out[l])

        if not last:
            oa_y = diff_attend(qa_y.reshape(B, C, DA_HEADS, 2, DA_HEAD_DIM), ka_y, va_y,
                               lam, da_subln_g[l], lam_init)
            ob_y = context_attention(qn_y.reshape(B, C, NA_HEADS, NA_HEAD_DIM), kn_y, vn_y)
            oc_y = fourier_mix(f_y)
            mix_y = merge_branches(oa_y, ob_y, oc_y, gt_y, w_branch[l], w_out[l])
            y = y + g1y * mix_y
            hy2 = rmsnorm(y, norm_ffn_g[l]) * (1.0 + sc2y) + sh2y
            y = y + g2y * moe_ffn(hy2, w_router[l], b_router[l], w_gate_up[l], b_gate_up[l],
                                  w_down[l], b_down[l])

        x = x + g1x * mix_x
        hx2 = rmsnorm(x, norm_ffn_g[l]) * (1.0 + sc2x) + sh2x
        x = x + g2x * moe_ffn(hx2, w_router[l], b_router[l], w_gate_up[l], b_gate_up[l],
                              w_down[l], b_down[l])
    return rmsnorm(x, final_g)
```

```python
import functools
import math

import jax
import jax.numpy as jnp
from jax import lax
from jax.experimental import pallas as pl
from jax.experimental.pallas import tpu as pltpu

F32 = jnp.float32
BF16 = jnp.bfloat16

EPS = 1e-6
GRID_W = 64
N_MOD = 6
DA_HEADS = 4
DA_HEAD_DIM = 64
ROPE_BASE = 10000.0
NA_HEADS = 8
NA_HEAD_DIM = 64
NA_KH = 8
NA_KW = 16
FT_GROUPS = 4
FT_GROUP_DIM = 128
BRANCH_WIDTH = 512
N_BRANCHES = 3
N_EXPERTS = 32
TOP_K = 4
SWIGLU_LIMIT = 7.0
SWIGLU_ALPHA = 1.702

LANES = 128
MOE_BLOCK = 256
NEG_BIG = -1e30
MIB = 1024 * 1024

COL_GATE = 0
COL_QA = 3072
COL_KA = 3584
COL_VA = 4096
COL_QN = 4608
COL_KN = 5120
COL_VN = 5632
COL_F = 6144
IN_COLS = 6656


def _params(sem, vmem_mib):
    return pltpu.CompilerParams(dimension_semantics=sem, vmem_limit_bytes=vmem_mib * MIB)


def _sigmoid(v):
    return 1.0 / (1.0 + jnp.exp(-v))


def _dot(a, b):
    return jnp.dot(a, b, preferred_element_type=F32)


def _dot_nt(a, b):
    return lax.dot_general(a, b, (((1,), (1,)), ((), ())), preferred_element_type=F32)


def _mod_kernel(c_ref, w_ref, b_ref, o_ref):
    c = c_ref[...]
    s = c * _sigmoid(c)
    o_ref[0] = _dot(s.astype(BF16), w_ref[0].astype(BF16)) + b_ref[0]


def _modulation(c_rows, w_mod, b_mod):
    depth, d, n = w_mod.shape
    rows = c_rows.shape[0]
    tn = 1536
    return pl.pallas_call(
        _mod_kernel,
        out_shape=jax.ShapeDtypeStruct((depth, rows, n), F32),
        grid=(depth, n // tn),
        in_specs=[
            pl.BlockSpec((rows, d), lambda l, j: (0, 0)),
            pl.BlockSpec((1, d, tn), lambda l, j: (l, 0, j)),
            pl.BlockSpec((1, 1, tn), lambda l, j: (l, 0, j)),
        ],
        out_specs=pl.BlockSpec((1, rows, tn), lambda l, j: (l, 0, j)),
        compiler_params=_params(("arbitrary", "arbitrary"), 40),
        name="modulation",
    )(c_rows, w_mod, b_mod.reshape(depth, 1, n))


def _rope128(v, cos, sin_signed, first_half):
    partner = jnp.where(first_half, pltpu.roll(v, LANES - 16, axis=1), pltpu.roll(v, 16, axis=1))
    return v * cos + partner * sin_signed


def _in_proj_kernel(x_ref, g_ref, sc_ref, sh_ref, w_ref, cos_ref, sin_ref, o_ref, h_scr, *, rope, tn):
    j = pl.program_id(1)

    @pl.when(j == 0)
    def _():
        x = x_ref[...]
        ms = jnp.mean(x * x, axis=-1, keepdims=True)
        h = x * lax.rsqrt(ms + EPS) * g_ref[...]
        h = h * (1.0 + sc_ref[0]) + sh_ref[0]
        h_scr[...] = h.astype(BF16)

    acc = _dot(h_scr[...], w_ref[...])
    if rope:
        jq = COL_QA // tn
        jk = COL_KA // tn
        is_rot = jnp.logical_or(j == jq, j == jk)

        @pl.when(is_rot)
        def _():
            cos = cos_ref[...]
            sin = sin_ref[...]
            lane = lax.broadcasted_iota(jnp.int32, (1, LANES), 1)
            first_half = (lane % 32) < 16
            pieces = [
                _rope128(acc[:, k * LANES:(k + 1) * LANES], cos, sin, first_half)
                for k in range(tn // LANES)
            ]
            o_ref[...] = jnp.concatenate(pieces, axis=1).astype(BF16)

        @pl.when(jnp.logical_not(is_rot))
        def _():
            o_ref[...] = acc.astype(BF16)
    else:
        o_ref[...] = acc.astype(BF16)


def _in_proj(x_flat, g, sc, sh, w_bf16, cos_t, sin_t, *, rows_per_mod, seq, rope, tm):
    t, d = x_flat.shape
    n = w_bf16.shape[1]
    tn = 512
    bpm = rows_per_mod // tm
    bps = seq // tm
    kern = functools.partial(_in_proj_kernel, rope=rope, tn=tn)
    return pl.pallas_call(
        kern,
        out_shape=jax.ShapeDtypeStruct((t, n), BF16),
        grid=(t // tm, n // tn),
        in_specs=[
            pl.BlockSpec((tm, d), lambda i, j: (i, 0)),
            pl.BlockSpec((1, d), lambda i, j: (0, 0)),
            pl.BlockSpec((1, 1, d), lambda i, j: (i // bpm, 0, 0)),
            pl.BlockSpec((1, 1, d), lambda i, j: (i // bpm, 0, 0)),
            pl.BlockSpec((d, tn), lambda i, j: (0, j)),
            pl.BlockSpec((tm, LANES), lambda i, j: (i % bps, 0)),
            pl.BlockSpec((tm, LANES), lambda i, j: (i % bps, 0)),
        ],
        out_specs=pl.BlockSpec((tm, tn), lambda i, j: (i, j)),
        scratch_shapes=[pltpu.VMEM((tm, d), BF16)],
        compiler_params=_params(("arbitrary", "arbitrary"), 48),
        name="in_proj_rope" if rope else "in_proj",
    )(x_flat, g.reshape(1, d), sc, sh, w_bf16, cos_t, sin_t)


def _softmax_parts(parts):
    m = functools.reduce(jnp.maximum, [jnp.max(p, axis=-1, keepdims=True) for p in parts])
    es = [jnp.exp(p - m) for p in parts]
    l = functools.reduce(lambda a, b: a + b, [jnp.sum(e, axis=-1, keepdims=True) for e in es])
    inv = 1.0 / l
    return [e * inv for e in es]


def _diff_attn_kernel(*refs, lam_init, n_kv):
    q_ref = refs[0]
    k_refs = refs[1:1 + n_kv]
    v_refs = refs[1 + n_kv:1 + 2 * n_kv]
    lam_ref, g_ref, o_ref = refs[1 + 2 * n_kv:]
    scale = DA_HEAD_DIM ** -0.5
    q = q_ref[0]
    lane = lax.broadcasted_iota(jnp.int32, (1, LANES), 1)
    m1 = jnp.where(lane < DA_HEAD_DIM, 1.0, 0.0).astype(BF16)
    m2 = jnp.where(lane >= DA_HEAD_DIM, 1.0, 0.0).astype(BF16)
    ks = [k_ref[0] for k_ref in k_refs]
    p1 = _softmax_parts([_dot_nt(q * m1, k) * scale for k in ks])
    p2 = _softmax_parts([_dot_nt(q * m2, k) * scale for k in ks])
    lp = lam_ref[...]
    t1 = jnp.sum(lp[0:1] * lp[1:2], axis=-1, keepdims=True)
    t2 = jnp.sum(lp[2:3] * lp[3:4], axis=-1, keepdims=True)
    lam = jnp.exp(t1) - jnp.exp(t2) + lam_init
    o = None
    for a1, a2, v_ref in zip(p1, p2, v_refs):
        a = (a1 - lam * a2).astype(BF16)
        part = _dot(a, v_ref[0])
        o = part if o is None else o + part
    ms = jnp.mean(o * o, axis=-1, keepdims=True)
    o = o * lax.rsqrt(ms + EPS) * g_ref[...] * (1.0 - lam_init)
    o_ref[0] = o.astype(BF16)


def _diff_attention(p_q, kv_sources, lam_params, subln_g, lam_init, *, tq):
    b, nq, _ = p_q.shape
    qa0, ka0, va0 = COL_QA // LANES, COL_KA // LANES, COL_VA // LANES
    n_kv = len(kv_sources)
    in_specs = [pl.BlockSpec((1, tq, LANES), lambda bi, h, i: (bi, i, qa0 + h))]
    for src in kv_sources:
        in_specs.append(pl.BlockSpec((1, src.shape[1], LANES), lambda bi, h, i: (bi, 0, ka0 + h)))
    for src in kv_sources:
        in_specs.append(pl.BlockSpec((1, src.shape[1], LANES), lambda bi, h, i: (bi, 0, va0 + h)))
    in_specs.append(pl.BlockSpec((4, DA_HEAD_DIM), lambda bi, h, i: (0, 0)))
    in_specs.append(pl.BlockSpec((1, LANES), lambda bi, h, i: (0, 0)))
    kern = functools.partial(_diff_attn_kernel, lam_init=lam_init, n_kv=n_kv)
    return pl.pallas_call(
        kern,
        out_shape=jax.ShapeDtypeStruct((b, nq, DA_HEADS * LANES), BF16),
        grid=(b, DA_HEADS, nq // tq),
        in_specs=in_specs,
        out_specs=pl.BlockSpec((1, tq, LANES), lambda bi, h, i: (bi, i, h)),
        compiler_params=_params(("arbitrary", "arbitrary", "arbitrary"), 48),
        name="diff_attention",
    )(p_q, *kv_sources, *kv_sources, lam_params, subln_g.reshape(1, LANES))


def _head_pair_attention(q_pair, score_fn, value_fn):
    lane = lax.broadcasted_iota(jnp.int32, (1, LANES), 1)
    outs = []
    for hh in range(2):
        sel = jnp.where((lane // NA_HEAD_DIM) == hh, 1.0, 0.0).astype(BF16)
        qm = q_pair * sel
        probs = _softmax_parts(score_fn(qm, hh))
        outs.append(value_fn([p.astype(BF16) for p in probs]))
    return jnp.where(lane < NA_HEAD_DIM, outs[0], outs[1])


def _nbr_attn_kernel(q_ref, kx_ref, vx_ref, kc_ref, vc_ref, bias_ref, o_ref, *, rows):
    r = pl.program_id(1)
    scale = NA_HEAD_DIM ** -0.5
    rs = jnp.clip(r - NA_KH // 2, 0, rows - NA_KH)
    dr0 = rs - r + NA_KH - 1
    win = NA_KH * GRID_W
    start = pl.multiple_of(rs * GRID_W, GRID_W)
    for p in range(NA_HEADS // 2):
        cols = slice(p * LANES, (p + 1) * LANES)
        k_win = kx_ref[0, pl.ds(start, win), cols]
        v_win = vx_ref[0, pl.ds(start, win), cols]
        k_ctx = kc_ref[0, :, cols]
        v_ctx = vc_ref[0, :, cols]

        def score_fn(qm, hh, k_win=k_win, k_ctx=k_ctx, p=p):
            s_win = _dot_nt(qm, k_win) * scale
            bias = jnp.concatenate(
                [bias_ref[2 * p + hh, dr0 + 2 * jj] for jj in range(NA_KH // 2)], axis=1)
            return [s_win + bias, _dot_nt(qm, k_ctx) * scale]

        def value_fn(probs, v_win=v_win, v_ctx=v_ctx):
            return _dot(probs[0], v_win) + _dot(probs[1], v_ctx)

        o = _head_pair_attention(q_ref[0, :, cols], score_fn, value_fn)
        o_ref[0, :, cols] = o.astype(BF16)


def _nbr_attention(p_x, p_y, bias_t2):
    b, s, _ = p_x.shape
    c = p_y.shape[1]
    rows = s // GRID_W
    w = NA_HEADS * NA_HEAD_DIM
    jq, jk, jv = COL_QN // w, COL_KN // w, COL_VN // w
    kern = functools.partial(_nbr_attn_kernel, rows=rows)
    return pl.pallas_call(
        kern,
        out_shape=jax.ShapeDtypeStruct((b, s, w), BF16),
        grid=(b, rows),
        in_specs=[
            pl.BlockSpec((1, GRID_W, w), lambda bi, r: (bi, r, jq)),
            pl.BlockSpec((1, s, w), lambda bi, r: (bi, 0, jk)),
            pl.BlockSpec((1, s, w), lambda bi, r: (bi, 0, jv)),
            pl.BlockSpec((1, c, w), lambda bi, r: (bi, 0, jk)),
            pl.BlockSpec((1, c, w), lambda bi, r: (bi, 0, jv)),
            pl.BlockSpec(bias_t2.shape, lambda bi, r: (0, 0, 0, 0)),
        ],
        out_specs=pl.BlockSpec((1, GRID_W, w), lambda bi, r: (bi, r, 0)),
        compiler_params=_params(("arbitrary", "arbitrary"), 48),
        name="nbr_attention",
    )(p_x, p_x, p_x, p_y, p_y, bias_t2)


def _ctx_attn_kernel(q_ref, k_ref, v_ref, o_ref):
    scale = NA_HEAD_DIM ** -0.5
    for p in range(NA_HEADS // 2):
        cols = slice(p * LANES, (p + 1) * LANES)
        k = k_ref[0, :, cols]
        v = v_ref[0, :, cols]
        o = _head_pair_attention(
            q_ref[0, :, cols],
            lambda qm, hh, k=k: [_dot_nt(qm, k) * scale],
            lambda probs, v=v: _dot(probs[0], v))
        o_ref[0, :, cols] = o.astype(BF16)


def _ctx_attention(p_y):
    b, c, _ = p_y.shape
    w = NA_HEADS * NA_HEAD_DIM
    jq, jk, jv = COL_QN // w, COL_KN // w, COL_VN // w
    return pl.pallas_call(
        _ctx_attn_kernel,
        out_shape=jax.ShapeDtypeStruct((b, c, w), BF16),
        grid=(b,),
        in_specs=[
            pl.BlockSpec((1, c, w), lambda bi: (bi, 0, jq)),
            pl.BlockSpec((1, c, w), lambda bi: (bi, 0, jk)),
            pl.BlockSpec((1, c, w), lambda bi: (bi, 0, jv)),
        ],
        out_specs=pl.BlockSpec((1, c, w), lambda bi: (bi, 0, 0)),
        compiler_params=_params(("arbitrary",), 32),
        name="ctx_attention",
    )(p_y, p_y, p_y)


def _fourier_kernel(f_ref, cc_ref, sc_ref, cn_ref, sn_ref, o_ref, ab_scr, *, norm):
    i = pl.program_id(1)
    w = FT_GROUPS * FT_GROUP_DIM

    @pl.when(i == 0)
    def _():
        for g in range(FT_GROUPS):
            u = f_ref[0, :, g * FT_GROUP_DIM:(g + 1) * FT_GROUP_DIM]
            ab_scr[:, g * FT_GROUP_DIM:(g + 1) * FT_GROUP_DIM] = _dot(u, cc_ref[...]).astype(BF16)
            ab_scr[:, w + g * FT_GROUP_DIM:w + (g + 1) * FT_GROUP_DIM] = _dot(u, sc_ref[...]).astype(BF16)

    o = _dot(cn_ref[...], ab_scr[:, :w]) - _dot(sn_ref[...], ab_scr[:, w:])
    o_ref[0] = (o * norm).astype(BF16)


def _dft_tables(n):
    j = jnp.arange(n, dtype=jnp.int32)
    m = (j[:, None] * j[None, :]) % n
    ang = m.astype(F32) * (2.0 * math.pi / n)
    return jnp.cos(ang).astype(BF16), jnp.sin(ang).astype(BF16)


def _fourier_mix(p, tables_n, tables_c, *, tm):
    b, n, _ = p.shape
    w = FT_GROUPS * FT_GROUP_DIM
    cn, sn = tables_n
    cc, sc = tables_c
    jf = COL_F // w
    kern = functools.partial(_fourier_kernel, norm=1.0 / math.sqrt(n * FT_GROUP_DIM))
    return pl.pallas_call(
        kern,
        out_shape=jax.ShapeDtypeStruct((b, n, w), BF16),
        grid=(b, n // tm),
        in_specs=[
            pl.BlockSpec((1, n, w), lambda bi, i: (bi, 0, jf)),
            pl.BlockSpec(cc.shape, lambda bi, i: (0, 0)),
            pl.BlockSpec(sc.shape, lambda bi, i: (0, 0)),
            pl.BlockSpec((tm, n), lambda bi, i: (i, 0)),
            pl.BlockSpec((tm, n), lambda bi, i: (i, 0)),
        ],
        out_specs=pl.BlockSpec((1, tm, w), lambda bi, i: (bi, i, 0)),
        scratch_shapes=[pltpu.VMEM((n, 2 * w), BF16)],
        compiler_params=_params(("arbitrary", "arbitrary"), 48),
        name="fourier_mix",
    )(p, cc, sc, cn, sn)


def _merge_kernel(oa_ref, ob_ref, oc_ref, gt_ref, x_ref, g1_ref, wb_ref, wo_ref, gn_ref, sc_ref,
                  sh_ref, wr_ref, br_ref, xo_ref, h_ref, ti_ref, tw_ref):
    d = x_ref.shape[1]
    m = None
    for i, o_ref in enumerate((oa_ref, ob_ref, oc_ref)):
        proj = _dot(o_ref[...], wb_ref[i])
        gate = _sigmoid(gt_ref[:, i * d:(i + 1) * d].astype(F32))
        m = gate * proj if m is None else m + gate * proj
    mix = _dot(m.astype(BF16), wo_ref[...])
    x = x_ref[...] + g1_ref[0] * mix
    xo_ref[...] = x
    ms = jnp.mean(x * x, axis=-1, keepdims=True)
    h = x * lax.rsqrt(ms + EPS) * gn_ref[...]
    h = h * (1.0 + sc_ref[0]) + sh_ref[0]
    h_ref[...] = h
    logits = jnp.dot(h, wr_ref[...], precision=lax.Precision.HIGHEST,
                     preferred_element_type=F32) + br_ref[...]
    lane = lax.broadcasted_iota(jnp.int32, logits.shape, 1).astype(F32)
    cur = logits
    vals, idxs = [], []
    for _ in range(TOP_K):
        mx = jnp.max(cur, axis=-1, keepdims=True)
        ix = jnp.min(jnp.where(cur == mx, lane, float(LANES)), axis=-1, keepdims=True)
        vals.append(mx)
        idxs.append(ix)
        cur = jnp.where(lane == ix, -jnp.inf, cur)
    es = [jnp.exp(v - vals[0]) for v in vals]
    inv = 1.0 / functools.reduce(lambda a, b: a + b, es)
    ti = jnp.zeros(logits.shape, F32)
    tw = jnp.zeros(logits.shape, F32)
    for k in range(TOP_K):
        ti = jnp.where(lane == float(k), idxs[k], ti)
        tw = jnp.where(lane == float(k), es[k] * inv, tw)
    ti_ref[...] = ti.astype(jnp.int32)
    tw_ref[...] = tw


def _merge(oa, ob, oc, p_flat, x_flat, g1, w_branch, w_out, g_ffn, sc2, sh2, w_router, b_router,
           *, rows_per_mod, tm):
    t, d = x_flat.shape
    bw = BRANCH_WIDTH
    bpm = rows_per_mod // tm
    row = lambda i: (i, 0)
    mod = lambda i: (i // bpm, 0, 0)
    whole2 = lambda i: (0, 0)
    return pl.pallas_call(
        _merge_kernel,
        out_shape=(
            jax.ShapeDtypeStruct((t, d), F32),
            jax.ShapeDtypeStruct((t, d), F32),
            jax.ShapeDtypeStruct((t, LANES), jnp.int32),
            jax.ShapeDtypeStruct((t, LANES), F32),
        ),
        grid=(t // tm,),
        in_specs=[
            pl.BlockSpec((tm, bw), row),
            pl.BlockSpec((tm, bw), row),
            pl.BlockSpec((tm, bw), row),
            pl.BlockSpec((tm, N_BRANCHES * d), row),
            pl.BlockSpec((tm, d), row),
            pl.BlockSpec((1, 1, d), mod),
            pl.BlockSpec((N_BRANCHES, bw, d), lambda i: (0, 0, 0)),
            pl.BlockSpec((d, d), whole2),
            pl.BlockSpec((1, d), whole2),
            pl.BlockSpec((1, 1, d), mod),
            pl.BlockSpec((1, 1, d), mod),
            pl.BlockSpec((d, LANES), whole2),
            pl.BlockSpec((1, LANES), whole2),
        ],
        out_specs=(
            pl.BlockSpec((tm, d), row),
            pl.BlockSpec((tm, d), row),
            pl.BlockSpec((tm, LANES), row),
            pl.BlockSpec((tm, LANES), row),
        ),
        compiler_params=_params(("arbitrary",), 48),
        name="merge_route",
    )(oa, ob, oc, p_flat, x_flat, g1, w_branch, w_out, g_ffn.reshape(1, d), sc2, sh2,
      w_router, b_router)


def _expert_kernel(be_ref, nu_ref, tok_ref, h_hbm, wgu_ref, bgu_ref, wd_ref, bd_ref, y_ref,
                   xbuf, sem):
    i = pl.program_id(0)
    blk = xbuf.shape[0]
    f = wd_ref.shape[1]

    @pl.when(i < nu_ref[0])
    def _():
        def issue(r, carry):
            tok = tok_ref[0, 0, r]
            pltpu.make_async_copy(h_hbm.at[pl.ds(tok, 1)], xbuf.at[pl.ds(r, 1)], sem).start()
            return carry

        lax.fori_loop(0, blk, issue, 0)
        pltpu.make_async_copy(h_hbm.at[pl.ds(0, blk)], xbuf, sem).wait()
        x = xbuf[...].astype(BF16)
        gu = _dot(x, wgu_ref[0]) + bgu_ref[0]
        gate = jnp.minimum(gu[:, :f], SWIGLU_LIMIT)
        up = jnp.clip(gu[:, f:], -SWIGLU_LIMIT, SWIGLU_LIMIT)
        act = gate * _sigmoid(SWIGLU_ALPHA * gate) * (up + 1.0)
        y_ref[...] = _dot(act.astype(BF16), wd_ref[0]) + bd_ref[0]

    @pl.when(i >= nu_ref[0])
    def _():
        y_ref[...] = jnp.zeros(y_ref.shape, y_ref.dtype)


def _expert_ffn(h, slot_tok, block_e, n_used, wgu, bgu, wd, bd):
    t, d = h.shape
    n_blocks = block_e.shape[0]
    e, _, f2 = wgu.shape
    f = f2 // 2
    grid_spec = pltpu.PrefetchScalarGridSpec(
        num_scalar_prefetch=2,
        grid=(n_blocks,),
        in_specs=[
            pl.BlockSpec((1, 1, MOE_BLOCK), lambda i, be, nu: (i, 0, 0), memory_space=pltpu.SMEM),
            pl.BlockSpec(memory_space=pl.ANY),
            pl.BlockSpec((1, d, f2), lambda i, be, nu: (be[i], 0, 0)),
            pl.BlockSpec((1, 1, f2), lambda i, be, nu: (be[i], 0, 0)),
            pl.BlockSpec((1, f, d), lambda i, be, nu: (be[i], 0, 0)),
            pl.BlockSpec((1, 1, d), lambda i, be, nu: (be[i], 0, 0)),
        ],
        out_specs=pl.BlockSpec((MOE_BLOCK, d), lambda i, be, nu: (i, 0)),
        scratch_shapes=[pltpu.VMEM((MOE_BLOCK, d), F32), pltpu.SemaphoreType.DMA(())],
    )
    return pl.pallas_call(
        _expert_kernel,
        out_shape=jax.ShapeDtypeStruct((n_blocks * MOE_BLOCK, d), F32),
        grid_spec=grid_spec,
        compiler_params=_params(("arbitrary",), 48),
        name="expert_ffn",
    )(block_e, n_used, slot_tok.reshape(n_blocks, 1, MOE_BLOCK), h, wgu,
      bgu.reshape(e, 1, f2), wd, bd.reshape(e, 1, d))


def _combine_kernel(slot_ref, y_hbm, tw_ref, x_ref, g2_ref, gf_ref, o_ref, ybuf, sem, *, final):
    tm = x_ref.shape[0]

    def issue(r, carry):
        for k in range(TOP_K):
            s = slot_ref[0, 0, r * TOP_K + k]
            pltpu.make_async_copy(y_hbm.at[pl.ds(s, 1)], ybuf.at[k, pl.ds(r, 1)], sem).start()
        return carry

    lax.fori_loop(0, tm, issue, 0)
    for k in range(TOP_K):
        pltpu.make_async_copy(y_hbm.at[pl.ds(0, tm)], ybuf.at[k], sem).wait()
    tw = tw_ref[...]
    lane = lax.broadcasted_iota(jnp.int32, tw.shape, 1)
    moe = None
    for k in range(TOP_K):
        wk = jnp.sum(jnp.where(lane == k, tw, 0.0), axis=-1, keepdims=True)
        term = wk * ybuf[k]
        moe = term if moe is None else moe + term
    x = x_ref[...] + g2_ref[0] * moe
    if final:
        ms = jnp.mean(x * x, axis=-1, keepdims=True)
        x = x * lax.rsqrt(ms + EPS) * gf_ref[...]
    o_ref[...] = x


def _combine(slots, y_slots, top_w, x_flat, g2, final_g, *, rows_per_mod, final, tm):
    t, d = x_flat.shape
    bpm = rows_per_mod // tm
    kern = functools.partial(_combine_kernel, final=final)
    return pl.pallas_call(
        kern,
        out_shape=jax.ShapeDtypeStruct((t, d), F32),
        grid=(t // tm,),
        in_specs=[
            pl.BlockSpec((1, 1, tm * TOP_K), lambda i: (i, 0, 0), memory_space=pltpu.SMEM),
            pl.BlockSpec(memory_space=pl.ANY),
            pl.BlockSpec((tm, LANES), lambda i: (i, 0)),
            pl.BlockSpec((tm, d), lambda i: (i, 0)),
            pl.BlockSpec((1, 1, d), lambda i: (i // bpm, 0, 0)),
            pl.BlockSpec((1, d), lambda i: (0, 0)),
        ],
        out_specs=pl.BlockSpec((tm, d), lambda i: (i, 0)),
        scratch_shapes=[pltpu.VMEM((TOP_K, tm, d), F32), pltpu.SemaphoreType.DMA(())],
        compiler_params=_params(("arbitrary",), 32),
        name="moe_combine",
    )(slots.reshape(t // tm, 1, tm * TOP_K), y_slots, top_w, x_flat, g2, final_g.reshape(1, d))


def _route_plan(top_i, t):
    flat_e = top_i[:, :TOP_K].reshape(-1)
    n_assign = flat_e.shape[0]
    onehot = (flat_e[:, None] == jnp.arange(N_EXPERTS, dtype=jnp.int32)[None, :]).astype(jnp.int32)
    csum = jnp.cumsum(onehot, axis=0)
    rank = jnp.sum((csum - onehot) * onehot, axis=1)
    counts = csum[-1]
    nblk = (counts + MOE_BLOCK - 1) // MOE_BLOCK
    blk_end = jnp.cumsum(nblk)
    blk_start = blk_end - nblk
    slot = blk_start[flat_e] * MOE_BLOCK + rank
    n_blocks = -(-n_assign // MOE_BLOCK) + N_EXPERTS
    block_e = jnp.minimum(
        jnp.searchsorted(blk_end, jnp.arange(n_blocks, dtype=jnp.int32), side='right'),
        N_EXPERTS - 1).astype(jnp.int32)
    tok = (jnp.arange(n_assign, dtype=jnp.int32) // TOP_K)
    slot_tok = jnp.zeros((n_blocks * MOE_BLOCK,), jnp.int32).at[slot].set(tok)
    return slot.astype(jnp.int32), slot_tok, block_e, blk_end[-1:].astype(jnp.int32)


def _moe(h, top_i, top_w, x_flat, g2, final_g, wgu, bgu, wd, bd, *, rows_per_mod, final):
    t = h.shape[0]
    slot, slot_tok, block_e, n_used = _route_plan(top_i, t)
    y_slots = _expert_ffn(h, slot_tok, block_e, n_used, wgu, bgu, wd, bd)
    return _combine(slot, y_slots, top_w, x_flat, g2, final_g,
                    rows_per_mod=rows_per_mod, final=final, tm=256)


def _rope_tables(n):
    t = jnp.arange(n)
    row = (t // GRID_W).astype(F32)
    col = (t % GRID_W).astype(F32)
    quarter = DA_HEAD_DIM // 4
    freqs = ROPE_BASE ** (-jnp.arange(quarter, dtype=F32) / quarter)
    lane = jnp.arange(LANES)
    pos = jnp.where(((lane % DA_HEAD_DIM) // (DA_HEAD_DIM // 2))[None, :] == 0, row[:, None], col[:, None])
    ang = pos * freqs[lane % quarter][None, :]
    sign = jnp.where((lane % (DA_HEAD_DIM // 2)) < quarter, -1.0, 1.0)[None, :]
    return jnp.cos(ang), jnp.sin(ang) * sign


def _nbr_bias_table(rpb):
    col = jnp.arange(GRID_W)
    col_start = jnp.clip(col - NA_KW // 2, 0, GRID_W - NA_KW)
    mask = (col[None, :] >= col_start[:, None]) & (col[None, :] < col_start[:, None] + NA_KW)
    dc = jnp.clip(col[None, :] - col[:, None], 1 - NA_KW, NA_KW - 1) + NA_KW - 1
    tab = jnp.where(mask[None, None], rpb.astype(F32)[:, :, dc], NEG_BIG)
    return jnp.concatenate([tab[:, :-1], tab[:, 1:]], axis=-1)


def _permute_in_cols(w_in_l):
    split = IN_COLS - N_BRANCHES * w_in_l.shape[0]
    return jnp.concatenate([w_in_l[:, split:], w_in_l[:, :split]], axis=1).astype(BF16)


def kernel(x, c, ctx, c_ctx, w_mod, b_mod, norm_mix_g, norm_ffn_g, w_in, da_lambda, da_subln_g,
           na_rpb, w_branch, w_out, w_router, b_router, w_gate_up, b_gate_up, w_down, b_down,
           final_g):
    b, s, d = x.shape
    cl = ctx.shape[1]
    depth = w_mod.shape[0]
    tx, ty = b * s, b * cl

    n_rows = -(-(b + 1) // 8) * 8
    c_rows = jnp.concatenate([c, c_ctx[None], jnp.zeros((n_rows - b - 1, d), F32)], axis=0)
    mod = _modulation(c_rows, w_mod, b_mod)

    cos_t, sin_t = _rope_tables(s)
    tab_s = _dft_tables(s)
    tab_c = _dft_tables(cl)
    tab_g = _dft_tables(FT_GROUP_DIM)

    xf = x.reshape(tx, d)
    yf = ctx.reshape(ty, d)
    for l in range(depth):
        last = l == depth - 1
        lam_init = 0.8 - 0.6 * math.exp(-0.3 * l)
        mods = mod[l].reshape(n_rows, N_MOD, d)
        mx = [mods[:b, i].reshape(b, 1, d) for i in range(N_MOD)]
        my = [mods[b:b + 1, i].reshape(1, 1, d) for i in range(N_MOD)]
        w_in_l = _permute_in_cols(w_in[l])
        wb_l = w_branch[l].astype(BF16)
        wo_l = w_out[l].astype(BF16)
        wr_l = jnp.pad(w_router[l], ((0, 0), (0, LANES - N_EXPERTS)))
        br_l = jnp.pad(b_router[l], (0, LANES - N_EXPERTS), constant_values=NEG_BIG).reshape(1, LANES)
        wgu_l = w_gate_up[l].astype(BF16)
        wd_l = w_down[l].astype(BF16)

        px = _in_proj(xf, norm_mix_g[l], mx[1], mx[0], w_in_l, cos_t, sin_t,
                      rows_per_mod=s, seq=s, rope=True, tm=1024)
        py = _in_proj(yf, norm_mix_g[l], my[1], my[0], w_in_l, cos_t, sin_t,
                      rows_per_mod=ty, seq=ty, rope=False, tm=min(1024, ty))
        px3 = px.reshape(b, s, IN_COLS)
        py3 = py.reshape(b, cl, IN_COLS)

        oa_x = _diff_attention(px3, [px3, py3], da_lambda[l], da_subln_g[l], lam_init, tq=256)
        ob_x = _nbr_attention(px3, py3, _nbr_bias_table(na_rpb[l]))
        oc_x = _fourier_mix(px3, tab_s, tab_g, tm=512)
        xf, hx, tix, twx = _merge(
            oa_x.reshape(tx, -1), ob_x.reshape(tx, -1), oc_x.reshape(tx, -1), px, xf, mx[2],
            wb_l, wo_l, norm_ffn_g[l], mx[4], mx[3], wr_l, br_l, rows_per_mod=s, tm=256)

        if not last:
            oa_y = _diff_attention(py3, [py3], da_lambda[l], da_subln_g[l], lam_init, tq=cl)
            ob_y = _ctx_attention(py3)
            oc_y = _fourier_mix(py3, tab_c, tab_g, tm=cl)
            yf, hy, tiy, twy = _merge(
                oa_y.reshape(ty, -1), ob_y.reshape(ty, -1), oc_y.reshape(ty, -1), py, yf, my[2],
                wb_l, wo_l, norm_ffn_g[l], my[4], my[3], wr_l, br_l, rows_per_mod=ty, tm=256)
            yf = _moe(hy, tiy, twy, yf, my[5], final_g, wgu_l, b_gate_up[l], wd_l, b_down[l],
                      rows_per_mod=ty, final=False)

        xf = _moe(hx, tix, twx, xf, mx[5], final_g, wgu_l, b_gate_up[l], wd_l, b_down[l],
                  rows_per_mod=s, final=last)
    return xf.reshape(b, s, d)
```

```python
import functools
import math

import jax
import jax.numpy as jnp
from jax import lax
from jax.experimental import pallas as pl
from jax.experimental.pallas import tpu as pltpu

F32 = jnp.float32
BF16 = jnp.bfloat16

EPS = 1e-6
GRID_W = 64
N_MOD = 6
DA_HEADS = 4
DA_HEAD_DIM = 64
ROPE_BASE = 10000.0
NA_HEADS = 8
NA_HEAD_DIM = 64
NA_KH = 8
NA_KW = 16
FT_GROUPS = 4
FT_GROUP_DIM = 128
BRANCH_WIDTH = 512
N_BRANCHES = 3
N_EXPERTS = 32
TOP_K = 4
SWIGLU_LIMIT = 7.0
SWIGLU_ALPHA = 1.702

LANES = 128
MOE_BLOCK = 512
NEG_BIG = -1e30
MIB = 1024 * 1024

COL_GATE = 0
COL_QA = 3072
COL_KA = 3584
COL_VA = 4096
COL_QN = 4608
COL_KN = 5120
COL_VN = 5632
COL_F = 6144
IN_COLS = 6656


def _params(sem, vmem_mib):
    return pltpu.CompilerParams(dimension_semantics=sem, vmem_limit_bytes=vmem_mib * MIB)


def _sigmoid(v):
    return 1.0 / (1.0 + jnp.exp(-v))


def _dot(a, b):
    return jnp.dot(a, b, preferred_element_type=F32)


def _dot_nt(a, b):
    return lax.dot_general(a, b, (((1,), (1,)), ((), ())), preferred_element_type=F32)


def _mod_kernel(c_ref, w_ref, b_ref, o_ref):
    c = c_ref[...]
    s = c * _sigmoid(c)
    o_ref[0] = _dot(s.astype(BF16), w_ref[0].astype(BF16)) + b_ref[0]


def _modulation(c_rows, w_mod, b_mod):
    depth, d, n = w_mod.shape
    rows = c_rows.shape[0]
    tn = 1536
    return pl.pallas_call(
        _mod_kernel,
        out_shape=jax.ShapeDtypeStruct((depth, rows, n), F32),
        grid=(depth, n // tn),
        in_specs=[
            pl.BlockSpec((rows, d), lambda l, j: (0, 0)),
            pl.BlockSpec((1, d, tn), lambda l, j: (l, 0, j)),
            pl.BlockSpec((1, 1, tn), lambda l, j: (l, 0, j)),
        ],
        out_specs=pl.BlockSpec((1, rows, tn), lambda l, j: (l, 0, j)),
        compiler_params=_params(("arbitrary", "arbitrary"), 40),
        name="modulation",
    )(c_rows, w_mod, b_mod.reshape(depth, 1, n))


def _rope128(v, cos, sin_signed, first_half):
    partner = jnp.where(first_half, pltpu.roll(v, LANES - 16, axis=1), pltpu.roll(v, 16, axis=1))
    return v * cos + partner * sin_signed


def _in_proj_kernel(x_ref, g_ref, sc_ref, sh_ref, w_ref, cos_ref, sin_ref, o_ref, h_scr, *, rope, tn):
    j = pl.program_id(1)

    @pl.when(j == 0)
    def _():
        x = x_ref[...]
        ms = jnp.mean(x * x, axis=-1, keepdims=True)
        h = x * lax.rsqrt(ms + EPS) * g_ref[...]
        h = h * (1.0 + sc_ref[0]) + sh_ref[0]
        h_scr[...] = h.astype(BF16)

    acc = _dot(h_scr[...], w_ref[...])
    if rope:
        jq = COL_QA // tn
        jk = COL_KA // tn
        is_rot = jnp.logical_or(j == jq, j == jk)

        @pl.when(is_rot)
        def _():
            cos = cos_ref[...]
            sin = sin_ref[...]
            lane = lax.broadcasted_iota(jnp.int32, (1, LANES), 1)
            first_half = (lane % 32) < 16
            pieces = [
                _rope128(acc[:, k * LANES:(k + 1) * LANES], cos, sin, first_half)
                for k in range(tn // LANES)
            ]
            o_ref[...] = jnp.concatenate(pieces, axis=1).astype(BF16)

        @pl.when(jnp.logical_not(is_rot))
        def _():
            o_ref[...] = acc.astype(BF16)
    else:
        o_ref[...] = acc.astype(BF16)


def _in_proj(x_flat, g, sc, sh, w_bf16, cos_t, sin_t, *, rows_per_mod, seq, rope, tm):
    t, d = x_flat.shape
    n = w_bf16.shape[1]
    tn = 512
    bpm = rows_per_mod // tm
    bps = seq // tm
    kern = functools.partial(_in_proj_kernel, rope=rope, tn=tn)
    return pl.pallas_call(
        kern,
        out_shape=jax.ShapeDtypeStruct((t, n), BF16),
        grid=(t // tm, n // tn),
        in_specs=[
            pl.BlockSpec((tm, d), lambda i, j: (i, 0)),
            pl.BlockSpec((1, d), lambda i, j: (0, 0)),
            pl.BlockSpec((1, 1, d), lambda i, j: (i // bpm, 0, 0)),
            pl.BlockSpec((1, 1, d), lambda i, j: (i // bpm, 0, 0)),
            pl.BlockSpec((d, tn), lambda i, j: (0, j)),
            pl.BlockSpec((tm, LANES), lambda i, j: (i % bps, 0)),
            pl.BlockSpec((tm, LANES), lambda i, j: (i % bps, 0)),
        ],
        out_specs=pl.BlockSpec((tm, tn), lambda i, j: (i, j)),
        scratch_shapes=[pltpu.VMEM((tm, d), BF16)],
        compiler_params=_params(("arbitrary", "arbitrary"), 48),
        name="in_proj_rope" if rope else "in_proj",
    )(x_flat, g.reshape(1, d), sc, sh, w_bf16, cos_t, sin_t)


def _diff_attn_kernel(*refs, lam_init, n_kv):
    q_ref = refs[0]
    k_refs = refs[1:1 + n_kv]
    v_refs = refs[1 + n_kv:1 + 2 * n_kv]
    lam_ref, g_ref, o_ref, vext_scr = refs[1 + 2 * n_kv:]
    sizes = [v_ref.shape[1] for v_ref in v_refs]
    offs = [sum(sizes[:n]) for n in range(n_kv)]

    @pl.when(pl.program_id(2) == 0)
    def _():
        for v_ref, off, n in zip(v_refs, offs, sizes):
            vext_scr[off:off + n, :LANES] = v_ref[0]
            vext_scr[off:off + n, LANES:] = jnp.ones((n, LANES), BF16)

    q = q_ref[0] * (DA_HEAD_DIM ** -0.5)
    lane = lax.broadcasted_iota(jnp.int32, (1, LANES), 1)
    masks = [jnp.where(lane < DA_HEAD_DIM, 1.0, 0.0).astype(BF16),
             jnp.where(lane >= DA_HEAD_DIM, 1.0, 0.0).astype(BF16)]
    ks = [k_ref[0] for k_ref in k_refs]
    scores = [[_dot_nt(q * m, k) for k in ks] for m in masks]
    maxes = [functools.reduce(jnp.maximum, [jnp.max(p, axis=-1, keepdims=True) for p in parts])
             for parts in scores]
    outs = []
    for parts, mx in zip(scores, maxes):
        oe = None
        for p, off, n in zip(parts, offs, sizes):
            term = _dot(jnp.exp(p - mx).astype(BF16), vext_scr[off:off + n, :])
            oe = term if oe is None else oe + term
        outs.append(oe[:, :LANES] * (1.0 / oe[:, LANES:LANES + 1]))
    lp = lam_ref[...]
    t1 = jnp.sum(lp[0:1] * lp[1:2], axis=-1, keepdims=True)
    t2 = jnp.sum(lp[2:3] * lp[3:4], axis=-1, keepdims=True)
    lam = jnp.exp(t1) - jnp.exp(t2) + lam_init
    o = outs[0] - lam * outs[1]
    ms = jnp.mean(o * o, axis=-1, keepdims=True)
    o = o * lax.rsqrt(ms + EPS) * g_ref[...] * (1.0 - lam_init)
    o_ref[0] = o.astype(BF16)


def _diff_attention(p_q, kv_sources, lam_params, subln_g, lam_init, *, tq):
    b, nq, _ = p_q.shape
    qa0, ka0, va0 = COL_QA // LANES, COL_KA // LANES, COL_VA // LANES
    n_kv = len(kv_sources)
    in_specs = [pl.BlockSpec((1, tq, LANES), lambda bi, h, i: (bi, i, qa0 + h))]
    for src in kv_sources:
        in_specs.append(pl.BlockSpec((1, src.shape[1], LANES), lambda bi, h, i: (bi, 0, ka0 + h)))
    for src in kv_sources:
        in_specs.append(pl.BlockSpec((1, src.shape[1], LANES), lambda bi, h, i: (bi, 0, va0 + h)))
    in_specs.append(pl.BlockSpec((4, DA_HEAD_DIM), lambda bi, h, i: (0, 0)))
    in_specs.append(pl.BlockSpec((1, LANES), lambda bi, h, i: (0, 0)))
    kern = functools.partial(_diff_attn_kernel, lam_init=lam_init, n_kv=n_kv)
    return pl.pallas_call(
        kern,
        out_shape=jax.ShapeDtypeStruct((b, nq, DA_HEADS * LANES), BF16),
        grid=(b, DA_HEADS, nq // tq),
        in_specs=in_specs,
        out_specs=pl.BlockSpec((1, tq, LANES), lambda bi, h, i: (bi, i, h)),
        scratch_shapes=[pltpu.VMEM((sum(src.shape[1] for src in kv_sources), 2 * LANES), BF16)],
        compiler_params=_params(("arbitrary", "arbitrary", "arbitrary"), 48),
        name="diff_attention",
    )(p_q, *kv_sources, *kv_sources, lam_params, subln_g.reshape(1, LANES))


def _head_masks():
    lane = lax.broadcasted_iota(jnp.int32, (1, LANES), 1)
    return [jnp.where((lane // NA_HEAD_DIM) == hh, 1.0, 0.0).astype(BF16) for hh in range(2)]


def _softmax_stage(score_parts_per_head):
    maxes = [functools.reduce(jnp.maximum, [jnp.max(p, axis=-1, keepdims=True) for p in parts])
             for parts in score_parts_per_head]
    exps = [[jnp.exp(p - m) for p in parts] for parts, m in zip(score_parts_per_head, maxes)]
    invs = [1.0 / functools.reduce(lambda a, b: a + b, [jnp.sum(e, axis=-1, keepdims=True) for e in es])
            for es in exps]
    return exps, invs


def _head_pair_attention(q_pair, score_fn, value_fn):
    lane = lax.broadcasted_iota(jnp.int32, (1, LANES), 1)
    masks = _head_masks()
    exps, invs = _softmax_stage([score_fn(q_pair * masks[hh], hh) for hh in range(2)])
    outs = [value_fn([e.astype(BF16) for e in es]) * inv for es, inv in zip(exps, invs)]
    return jnp.where(lane < NA_HEAD_DIM, outs[0], outs[1])


def _nbr_attn_kernel(q_ref, kx_ref, vx_ref, kc_ref, vc_ref, bias_ref, o_ref, *, rows):
    r = pl.program_id(1)
    scale = NA_HEAD_DIM ** -0.5
    rs = jnp.clip(r - NA_KH // 2, 0, rows - NA_KH)
    dr0 = rs - r + NA_KH - 1
    win = NA_KH * GRID_W
    start = pl.multiple_of(rs * GRID_W, GRID_W)
    lane = lax.broadcasted_iota(jnp.int32, (1, LANES), 1)
    masks = _head_masks()
    n_pairs = NA_HEADS // 2
    cols = [slice(p * LANES, (p + 1) * LANES) for p in range(n_pairs)]
    scores = []
    for p in range(n_pairs):
        q_pair = q_ref[0, :, cols[p]] * scale
        k_win = kx_ref[0, pl.ds(start, win), cols[p]]
        k_ctx = kc_ref[0, :, cols[p]]
        for hh in range(2):
            qm = q_pair * masks[hh]
            bias = jnp.concatenate(
                [bias_ref[2 * p + hh, dr0 + 2 * jj] for jj in range(NA_KH // 2)], axis=1)
            scores.append([_dot_nt(qm, k_win) + bias, _dot_nt(qm, k_ctx)])
    exps, invs = _softmax_stage(scores)
    for p in range(n_pairs):
        v_win = vx_ref[0, pl.ds(start, win), cols[p]]
        v_ctx = vc_ref[0, :, cols[p]]
        outs = []
        for hh in range(2):
            e_win, e_ctx = exps[2 * p + hh]
            o = _dot(e_win.astype(BF16), v_win) + _dot(e_ctx.astype(BF16), v_ctx)
            outs.append(o * invs[2 * p + hh])
        o_ref[0, :, cols[p]] = jnp.where(lane < NA_HEAD_DIM, outs[0], outs[1]).astype(BF16)


def _nbr_attention(p_x, p_y, bias_t2):
    b, s, _ = p_x.shape
    c = p_y.shape[1]
    rows = s // GRID_W
    w = NA_HEADS * NA_HEAD_DIM
    jq, jk, jv = COL_QN // w, COL_KN // w, COL_VN // w
    kern = functools.partial(_nbr_attn_kernel, rows=rows)
    return pl.pallas_call(
        kern,
        out_shape=jax.ShapeDtypeStruct((b, s, w), BF16),
        grid=(b, rows),
        in_specs=[
            pl.BlockSpec((1, GRID_W, w), lambda bi, r: (bi, r, jq)),
            pl.BlockSpec((1, s, w), lambda bi, r: (bi, 0, jk)),
            pl.BlockSpec((1, s, w), lambda bi, r: (bi, 0, jv)),
            pl.BlockSpec((1, c, w), lambda bi, r: (bi, 0, jk)),
            pl.BlockSpec((1, c, w), lambda bi, r: (bi, 0, jv)),
            pl.BlockSpec(bias_t2.shape, lambda bi, r: (0, 0, 0, 0)),
        ],
        out_specs=pl.BlockSpec((1, GRID_W, w), lambda bi, r: (bi, r, 0)),
        compiler_params=_params(("arbitrary", "arbitrary"), 48),
        name="nbr_attention",
    )(p_x, p_x, p_x, p_y, p_y, bias_t2)


def _ctx_attn_kernel(q_ref, k_ref, v_ref, o_ref):
    scale = NA_HEAD_DIM ** -0.5
    for p in range(NA_HEADS // 2):
        cols = slice(p * LANES, (p + 1) * LANES)
        k = k_ref[0, :, cols]
        v = v_ref[0, :, cols]
        o = _head_pair_attention(
            q_ref[0, :, cols],
            lambda qm, hh, k=k: [_dot_nt(qm, k) * scale],
            lambda probs, v=v: _dot(probs[0], v))
        o_ref[0, :, cols] = o.astype(BF16)


def _ctx_attention(p_y):
    b, c, _ = p_y.shape
    w = NA_HEADS * NA_HEAD_DIM
    jq, jk, jv = COL_QN // w, COL_KN // w, COL_VN // w
    return pl.pallas_call(
        _ctx_attn_kernel,
        out_shape=jax.ShapeDtypeStruct((b, c, w), BF16),
        grid=(b,),
        in_specs=[
            pl.BlockSpec((1, c, w), lambda bi: (bi, 0, jq)),
            pl.BlockSpec((1, c, w), lambda bi: (bi, 0, jk)),
            pl.BlockSpec((1, c, w), lambda bi: (bi, 0, jv)),
        ],
        out_specs=pl.BlockSpec((1, c, w), lambda bi: (bi, 0, 0)),
        compiler_params=_params(("arbitrary",), 32),
        name="ctx_attention",
    )(p_y, p_y, p_y)


def _fourier_kernel(f_ref, cc_ref, sc_ref, cn_ref, sn_ref, o_ref, ab_scr, *, norm):
    i = pl.program_id(1)
    w = FT_GROUPS * FT_GROUP_DIM

    @pl.when(i == 0)
    def _():
        for g in range(FT_GROUPS):
            u = f_ref[0, :, g * FT_GROUP_DIM:(g + 1) * FT_GROUP_DIM]
            ab_scr[:, g * FT_GROUP_DIM:(g + 1) * FT_GROUP_DIM] = _dot(u, cc_ref[...]).astype(BF16)
            ab_scr[:, w + g * FT_GROUP_DIM:w + (g + 1) * FT_GROUP_DIM] = _dot(u, sc_ref[...]).astype(BF16)

    o = _dot(cn_ref[...], ab_scr[:, :w]) - _dot(sn_ref[...], ab_scr[:, w:])
    o_ref[0] = (o * norm).astype(BF16)


def _dft_tables(n):
    j = jnp.arange(n, dtype=jnp.int32)
    m = (j[:, None] * j[None, :]) % n
    ang = m.astype(F32) * (2.0 * math.pi / n)
    return jnp.cos(ang).astype(BF16), jnp.sin(ang).astype(BF16)


def _fourier_mix(p, tables_n, tables_c, *, tm):
    b, n, _ = p.shape
    w = FT_GROUPS * FT_GROUP_DIM
    cn, sn = tables_n
    cc, sc = tables_c
    jf = COL_F // w
    kern = functools.partial(_fourier_kernel, norm=1.0 / math.sqrt(n * FT_GROUP_DIM))
    return pl.pallas_call(
        kern,
        out_shape=jax.ShapeDtypeStruct((b, n, w), BF16),
        grid=(b, n // tm),
        in_specs=[
            pl.BlockSpec((1, n, w), lambda bi, i: (bi, 0, jf)),
            pl.BlockSpec(cc.shape, lambda bi, i: (0, 0)),
            pl.BlockSpec(sc.shape, lambda bi, i: (0, 0)),
            pl.BlockSpec((tm, n), lambda bi, i: (i, 0)),
            pl.BlockSpec((tm, n), lambda bi, i: (i, 0)),
        ],
        out_specs=pl.BlockSpec((1, tm, w), lambda bi, i: (bi, i, 0)),
        scratch_shapes=[pltpu.VMEM((n, 2 * w), BF16)],
        compiler_params=_params(("arbitrary", "arbitrary"), 48),
        name="fourier_mix",
    )(p, cc, sc, cn, sn)


def _pack_bf16_pair(lo, hi):
    ulo = lax.bitcast_convert_type(lo.astype(BF16).astype(F32), jnp.uint32)
    uhi = lax.bitcast_convert_type(hi.astype(BF16).astype(F32), jnp.uint32)
    return uhi | (ulo >> 16)


def _unpack_bf16_pair(w):
    lo = lax.bitcast_convert_type(w << 16, F32)
    hi = lax.bitcast_convert_type(w & jnp.uint32(0xFFFF0000), F32)
    return lo, hi


def _merge_kernel(oa_ref, ob_ref, oc_ref, gt_ref, x_ref, g1_ref, wb_ref, wo_ref, gn_ref, sc_ref,
                  sh_ref, wr_ref, br_ref, xo_ref, h_ref, ti_ref, tw_ref, cnt_ref):
    d = x_ref.shape[1]
    tm = x_ref.shape[0]

    @pl.when(pl.program_id(0) == 0)
    def _():
        cnt_ref[...] = jnp.zeros(cnt_ref.shape, F32)
    m = None
    for i, o_ref in enumerate((oa_ref, ob_ref, oc_ref)):
        proj = _dot(o_ref[...], wb_ref[i])
        gate = _sigmoid(gt_ref[:, i * d:(i + 1) * d].astype(F32))
        m = gate * proj if m is None else m + gate * proj
    mix = _dot(m.astype(BF16), wo_ref[...])
    x = x_ref[...] + g1_ref[0] * mix
    xo_ref[...] = x
    ms = jnp.mean(x * x, axis=-1, keepdims=True)
    h = x * lax.rsqrt(ms + EPS) * gn_ref[...]
    h = h * (1.0 + sc_ref[0]) + sh_ref[0]
    h_ref[...] = _pack_bf16_pair(h[:, :d // 2], h[:, d // 2:])
    logits = jnp.dot(h, wr_ref[...], precision=lax.Precision.HIGHEST,
                     preferred_element_type=F32) + br_ref[...]
    lane = lax.broadcasted_iota(jnp.int32, logits.shape, 1).astype(F32)
    cur = logits
    vals, idxs = [], []
    for _ in range(TOP_K):
        mx = jnp.max(cur, axis=-1, keepdims=True)
        ix = jnp.min(jnp.where(cur == mx, lane, float(LANES)), axis=-1, keepdims=True)
        vals.append(mx)
        idxs.append(ix)
        cur = jnp.where(lane == ix, -jnp.inf, cur)
    es = [jnp.exp(v - vals[0]) for v in vals]
    inv = 1.0 / functools.reduce(lambda a, b: a + b, es)
    member = functools.reduce(lambda a, b: a + b, [jnp.where(lane == ix, 1.0, 0.0) for ix in idxs])
    row_i = lax.broadcasted_iota(jnp.int32, (tm, tm), 0)
    col_i = lax.broadcasted_iota(jnp.int32, (tm, tm), 1)
    lower = jnp.where(col_i < row_i, 1.0, 0.0).astype(BF16)
    prefix = _dot(lower, member.astype(BF16)) + cnt_ref[0:1, :]
    ti = jnp.zeros(logits.shape, F32)
    tw = jnp.zeros(logits.shape, F32)
    for k in range(TOP_K):
        rank = jnp.sum(jnp.where(lane == idxs[k], prefix, 0.0), axis=-1, keepdims=True)
        ti = jnp.where(lane == float(k), idxs[k], ti)
        ti = jnp.where(lane == float(TOP_K + k), rank, ti)
        tw = jnp.where(lane == float(k), es[k] * inv, tw)
    ti_ref[...] = ti.astype(jnp.int32)
    tw_ref[...] = tw
    cnt_ref[...] = cnt_ref[...] + jnp.sum(member, axis=0, keepdims=True)


def _merge(oa, ob, oc, p_flat, x_flat, g1, w_branch, w_out, g_ffn, sc2, sh2, w_router, b_router,
           *, rows_per_mod, tm):
    t, d = x_flat.shape
    bw = BRANCH_WIDTH
    bpm = rows_per_mod // tm
    row = lambda i: (i, 0)
    mod = lambda i: (i // bpm, 0, 0)
    whole2 = lambda i: (0, 0)
    return pl.pallas_call(
        _merge_kernel,
        out_shape=(
            jax.ShapeDtypeStruct((t, d), F32),
            jax.ShapeDtypeStruct((t, d // 2), jnp.uint32),
            jax.ShapeDtypeStruct((t, LANES), jnp.int32),
            jax.ShapeDtypeStruct((t, LANES), F32),
            jax.ShapeDtypeStruct((8, LANES), F32),
        ),
        grid=(t // tm,),
        in_specs=[
            pl.BlockSpec((tm, bw), row),
            pl.BlockSpec((tm, bw), row),
            pl.BlockSpec((tm, bw), row),
            pl.BlockSpec((tm, N_BRANCHES * d), row),
            pl.BlockSpec((tm, d), row),
            pl.BlockSpec((1, 1, d), mod),
            pl.BlockSpec((N_BRANCHES, bw, d), lambda i: (0, 0, 0)),
            pl.BlockSpec((d, d), whole2),
            pl.BlockSpec((1, d), whole2),
            pl.BlockSpec((1, 1, d), mod),
            pl.BlockSpec((1, 1, d), mod),
            pl.BlockSpec((d, LANES), whole2),
            pl.BlockSpec((1, LANES), whole2),
        ],
        out_specs=(
            pl.BlockSpec((tm, d), row),
            pl.BlockSpec((tm, d // 2), row),
            pl.BlockSpec((tm, LANES), row),
            pl.BlockSpec((tm, LANES), row),
            pl.BlockSpec((8, LANES), whole2),
        ),
        compiler_params=_params(("arbitrary",), 48),
        name="merge_route",
    )(oa, ob, oc, p_flat, x_flat, g1, w_branch, w_out, g_ffn.reshape(1, d), sc2, sh2,
      w_router, b_router)


def _dispatch_kernel(slot_ref, h_ref, xs_in, xs_out, sem):
    del xs_in
    tm = h_ref.shape[0]

    def issue(r, carry):
        for k in range(TOP_K):
            s = slot_ref[0, 0, r * TOP_K + k]
            pltpu.make_async_copy(h_ref.at[pl.ds(r, 1)], xs_out.at[pl.ds(s, 1)], sem).start()
        return carry

    lax.fori_loop(0, tm, issue, 0)
    for k in range(TOP_K):
        pltpu.make_async_copy(h_ref, xs_out.at[pl.ds(0, tm)], sem).wait()


def _dispatch(slots, h_packed, n_slots, *, tm):
    t, w = h_packed.shape
    return pl.pallas_call(
        _dispatch_kernel,
        out_shape=jax.ShapeDtypeStruct((n_slots, w), h_packed.dtype),
        grid=(t // tm,),
        in_specs=[
            pl.BlockSpec((1, 1, tm * TOP_K), lambda i: (i, 0, 0), memory_space=pltpu.SMEM),
            pl.BlockSpec((tm, w), lambda i: (i, 0)),
            pl.BlockSpec(memory_space=pl.ANY),
        ],
        out_specs=pl.BlockSpec(memory_space=pl.ANY),
        scratch_shapes=[pltpu.SemaphoreType.DMA(())],
        input_output_aliases={2: 0},
        compiler_params=_params(("arbitrary",), 32),
        name="moe_dispatch",
    )(slots.reshape(t // tm, 1, tm * TOP_K), h_packed, jnp.zeros((n_slots, w), h_packed.dtype))


def _expert_kernel(be_ref, nu_ref, x_ref, wgu_ref, bgu_ref, wd_ref, bd_ref, y_ref):
    i = pl.program_id(0)
    f = wd_ref.shape[1]

    @pl.when(i < nu_ref[0])
    def _():
        lo, hi = _unpack_bf16_pair(x_ref[...])
        x = jnp.concatenate([lo, hi], axis=1).astype(BF16)
        gu = _dot(x, wgu_ref[0]) + bgu_ref[0]
        gate = jnp.minimum(gu[:, :f], SWIGLU_LIMIT)
        up = jnp.clip(gu[:, f:], -SWIGLU_LIMIT, SWIGLU_LIMIT)
        act = gate * _sigmoid(SWIGLU_ALPHA * gate) * (up + 1.0)
        y_ref[...] = _dot(act.astype(BF16), wd_ref[0]) + bd_ref[0]

    @pl.when(i >= nu_ref[0])
    def _():
        y_ref[...] = jnp.zeros(y_ref.shape, y_ref.dtype)


def _expert_ffn(xs, block_e, n_used, wgu, bgu, wd, bd):
    n_slots, w = xs.shape
    n_blocks = block_e.shape[0]
    e, d, f2 = wgu.shape
    f = f2 // 2
    grid_spec = pltpu.PrefetchScalarGridSpec(
        num_scalar_prefetch=2,
        grid=(n_blocks,),
        in_specs=[
            pl.BlockSpec((MOE_BLOCK, w), lambda i, be, nu: (i, 0)),
            pl.BlockSpec((1, d, f2), lambda i, be, nu: (be[i], 0, 0)),
            pl.BlockSpec((1, 1, f2), lambda i, be, nu: (be[i], 0, 0)),
            pl.BlockSpec((1, f, d), lambda i, be, nu: (be[i], 0, 0)),
            pl.BlockSpec((1, 1, d), lambda i, be, nu: (be[i], 0, 0)),
        ],
        out_specs=pl.BlockSpec((MOE_BLOCK, d), lambda i, be, nu: (i, 0)),
    )
    return pl.pallas_call(
        _expert_kernel,
        out_shape=jax.ShapeDtypeStruct((n_slots, d), F32),
        grid_spec=grid_spec,
        compiler_params=_params(("arbitrary",), 48),
        name="expert_ffn",
    )(block_e, n_used, xs, wgu, bgu.reshape(e, 1, f2), wd, bd.reshape(e, 1, d))


def _combine_kernel(slot_ref, y_hbm, tw_ref, x_ref, g2_ref, gf_ref, o_ref, ybuf, sem, *, final):
    tm = x_ref.shape[0]

    def issue(r, carry):
        for k in range(TOP_K):
            s = slot_ref[0, 0, r * TOP_K + k]
            pltpu.make_async_copy(y_hbm.at[pl.ds(s, 1)], ybuf.at[k, pl.ds(r, 1)], sem).start()
        return carry

    lax.fori_loop(0, tm, issue, 0)
    for k in range(TOP_K):
        pltpu.make_async_copy(y_hbm.at[pl.ds(0, tm)], ybuf.at[k], sem).wait()
    tw = tw_ref[...]
    lane = lax.broadcasted_iota(jnp.int32, tw.shape, 1)
    moe = None
    for k in range(TOP_K):
        wk = jnp.sum(jnp.where(lane == k, tw, 0.0), axis=-1, keepdims=True)
        term = wk * ybuf[k]
        moe = term if moe is None else moe + term
    x = x_ref[...] + g2_ref[0] * moe
    if final:
        ms = jnp.mean(x * x, axis=-1, keepdims=True)
        x = x * lax.rsqrt(ms + EPS) * gf_ref[...]
    o_ref[...] = x


def _combine(slots, y_slots, top_w, x_flat, g2, final_g, *, rows_per_mod, final, tm):
    t, d = x_flat.shape
    bpm = rows_per_mod // tm
    kern = functools.partial(_combine_kernel, final=final)
    return pl.pallas_call(
        kern,
        out_shape=jax.ShapeDtypeStruct((t, d), F32),
        grid=(t // tm,),
        in_specs=[
            pl.BlockSpec((1, 1, tm * TOP_K), lambda i: (i, 0, 0), memory_space=pltpu.SMEM),
            pl.BlockSpec(memory_space=pl.ANY),
            pl.BlockSpec((tm, LANES), lambda i: (i, 0)),
            pl.BlockSpec((tm, d), lambda i: (i, 0)),
            pl.BlockSpec((1, 1, d), lambda i: (i // bpm, 0, 0)),
            pl.BlockSpec((1, d), lambda i: (0, 0)),
        ],
        out_specs=pl.BlockSpec((tm, d), lambda i: (i, 0)),
        scratch_shapes=[pltpu.VMEM((TOP_K, tm, d), F32), pltpu.SemaphoreType.DMA(())],
        compiler_params=_params(("arbitrary",), 32),
        name="moe_combine",
    )(slots.reshape(t // tm, 1, tm * TOP_K), y_slots, top_w, x_flat, g2, final_g.reshape(1, d))


def _route_plan(top_i, counts):
    t = top_i.shape[0]
    experts = jnp.arange(N_EXPERTS, dtype=jnp.int32)
    counts = counts.astype(jnp.int32)
    nblk = (counts + MOE_BLOCK - 1) // MOE_BLOCK
    blk_end = jnp.cumsum(nblk)
    base = (blk_end - nblk) * MOE_BLOCK
    e = top_i[:, :TOP_K]
    rank = top_i[:, TOP_K:2 * TOP_K]
    slot = jnp.sum(jnp.where(e[..., None] == experts, base, 0), axis=-1) + rank
    n_blocks = -(-(t * TOP_K) // MOE_BLOCK) + N_EXPERTS
    blocks = jnp.arange(n_blocks, dtype=jnp.int32)
    block_e = jnp.minimum(jnp.sum((blk_end[None, :] <= blocks[:, None]).astype(jnp.int32), axis=1),
                          N_EXPERTS - 1)
    return slot.reshape(-1), block_e, blk_end[-1:]


def _moe(h_packed, top_i, top_w, counts, x_flat, g2, final_g, wgu, bgu, wd, bd, *,
         rows_per_mod, final):
    slot, block_e, n_used = _route_plan(top_i, counts[0, :N_EXPERTS])
    xs = _dispatch(slot, h_packed, block_e.shape[0] * MOE_BLOCK, tm=min(512, h_packed.shape[0]))
    y_slots = _expert_ffn(xs, block_e, n_used, wgu, bgu, wd, bd)
    return _combine(slot, y_slots, top_w, x_flat, g2, final_g,
                    rows_per_mod=rows_per_mod, final=final, tm=256)


def _rope_tables(n):
    t = jnp.arange(n)
    row = (t // GRID_W).astype(F32)
    col = (t % GRID_W).astype(F32)
    quarter = DA_HEAD_DIM // 4
    freqs = ROPE_BASE ** (-jnp.arange(quarter, dtype=F32) / quarter)
    lane = jnp.arange(LANES)
    pos = jnp.where(((lane % DA_HEAD_DIM) // (DA_HEAD_DIM // 2))[None, :] == 0, row[:, None], col[:, None])
    ang = pos * freqs[lane % quarter][None, :]
    sign = jnp.where((lane % (DA_HEAD_DIM // 2)) < quarter, -1.0, 1.0)[None, :]
    return jnp.cos(ang), jnp.sin(ang) * sign


def _nbr_bias_table(rpb):
    col = jnp.arange(GRID_W)
    col_start = jnp.clip(col - NA_KW // 2, 0, GRID_W - NA_KW)
    mask = (col[None, :] >= col_start[:, None]) & (col[None, :] < col_start[:, None] + NA_KW)
    dc = jnp.clip(col[None, :] - col[:, None], 1 - NA_KW, NA_KW - 1) + NA_KW - 1
    tab = jnp.where(mask[None, None], rpb.astype(F32)[:, :, dc], NEG_BIG)
    return jnp.concatenate([tab[:, :-1], tab[:, 1:]], axis=-1)


def _permute_in_cols(w_in_l):
    split = IN_COLS - N_BRANCHES * w_in_l.shape[0]
    return jnp.concatenate([w_in_l[:, split:], w_in_l[:, :split]], axis=1).astype(BF16)


def kernel(x, c, ctx, c_ctx, w_mod, b_mod, norm_mix_g, norm_ffn_g, w_in, da_lambda, da_subln_g,
           na_rpb, w_branch, w_out, w_router, b_router, w_gate_up, b_gate_up, w_down, b_down,
           final_g):
    b, s, d = x.shape
    cl = ctx.shape[1]
    depth = w_mod.shape[0]
    tx, ty = b * s, b * cl

    n_rows = -(-(b + 1) // 8) * 8
    c_rows = jnp.concatenate([c, c_ctx[None], jnp.zeros((n_rows - b - 1, d), F32)], axis=0)
    mod = _modulation(c_rows, w_mod, b_mod)

    cos_t, sin_t = _rope_tables(s)
    tab_s = _dft_tables(s)
    tab_c = _dft_tables(cl)
    tab_g = _dft_tables(FT_GROUP_DIM)

    xf = x.reshape(tx, d)
    yf = ctx.reshape(ty, d)
    for l in range(depth):
        last = l == depth - 1
        lam_init = 0.8 - 0.6 * math.exp(-0.3 * l)
        mods = mod[l].reshape(n_rows, N_MOD, d)
        mx = [mods[:b, i].reshape(b, 1, d) for i in range(N_MOD)]
        my = [mods[b:b + 1, i].reshape(1, 1, d) for i in range(N_MOD)]
        w_in_l = _permute_in_cols(w_in[l])
        wb_l = w_branch[l].astype(BF16)
        wo_l = w_out[l].astype(BF16)
        wr_l = jnp.pad(w_router[l], ((0, 0), (0, LANES - N_EXPERTS)))
        br_l = jnp.pad(b_router[l], (0, LANES - N_EXPERTS), constant_values=NEG_BIG).reshape(1, LANES)
        wgu_l = w_gate_up[l].astype(BF16)
        wd_l = w_down[l].astype(BF16)

        px = _in_proj(xf, norm_mix_g[l], mx[1], mx[0], w_in_l, cos_t, sin_t,
                      rows_per_mod=s, seq=s, rope=True, tm=1024)
        py = _in_proj(yf, norm_mix_g[l], my[1], my[0], w_in_l, cos_t, sin_t,
                      rows_per_mod=ty, seq=ty, rope=False, tm=min(1024, ty))
        px3 = px.reshape(b, s, IN_COLS)
        py3 = py.reshape(b, cl, IN_COLS)

        oa_x = _diff_attention(px3, [px3, py3], da_lambda[l], da_subln_g[l], lam_init, tq=256)
        ob_x = _nbr_attention(px3, py3, _nbr_bias_table(na_rpb[l]))
        oc_x = _fourier_mix(px3, tab_s, tab_g, tm=512)
        xf, hx, tix, twx, cntx = _merge(
            oa_x.reshape(tx, -1), ob_x.reshape(tx, -1), oc_x.reshape(tx, -1), px, xf, mx[2],
            wb_l, wo_l, norm_ffn_g[l], mx[4], mx[3], wr_l, br_l, rows_per_mod=s, tm=256)

        if not last:
            oa_y = _diff_attention(py3, [py3], da_lambda[l], da_subln_g[l], lam_init, tq=cl)
            ob_y = _ctx_attention(py3)
            oc_y = _fourier_mix(py3, tab_c, tab_g, tm=cl)
            yf, hy, tiy, twy, cnty = _merge(
                oa_y.reshape(ty, -1), ob_y.reshape(ty, -1), oc_y.reshape(ty, -1), py, yf, my[2],
                wb_l, wo_l, norm_ffn_g[l], my[4], my[3], wr_l, br_l, rows_per_mod=ty, tm=256)
            yf = _moe(hy, tiy, twy, cnty, yf, my[5], final_g, wgu_l, b_gate_up[l], wd_l, b_down[l],
                      rows_per_mod=ty, final=False)

        xf = _moe(hx, tix, twx, cntx, xf, mx[5], final_g, wgu_l, b_gate_up[l], wd_l, b_down[l],
                  rows_per_mod=s, final=last)
    return xf.reshape(b, s, d)
```

```python
import functools
import math

import jax
import jax.numpy as jnp
from jax import lax
from jax.experimental import pallas as pl
from jax.experimental.pallas import tpu as pltpu

F32 = jnp.float32
BF16 = jnp.bfloat16

EPS = 1e-6
GRID_W = 64
N_MOD = 6
DA_HEADS = 4
DA_HEAD_DIM = 64
ROPE_BASE = 10000.0
NA_HEADS = 8
NA_HEAD_DIM = 64
NA_KH = 8
NA_KW = 16
FT_GROUPS = 4
FT_GROUP_DIM = 128
BRANCH_WIDTH = 512
N_BRANCHES = 3
N_EXPERTS = 32
TOP_K = 4
SWIGLU_LIMIT = 7.0
SWIGLU_ALPHA = 1.702

LANES = 128
MOE_BLOCK = 512
NEG_BIG = -1e30
MIB = 1024 * 1024

COL_QA = 0
COL_KA = 512
COL_VA = 1024
COL_QN = 1536
COL_KN = 2048
COL_VN = 2560
COL_F = 3072
COL_GATE = 3584
IN_COLS = 6656


def _params(sem, vmem_mib):
    return pltpu.CompilerParams(dimension_semantics=sem, vmem_limit_bytes=vmem_mib * MIB)


def _sigmoid(v):
    return 0.5 * jnp.tanh(0.5 * v) + 0.5


def _dot(a, b):
    return jnp.dot(a, b, preferred_element_type=F32)


def _dot_nt(a, b):
    return lax.dot_general(a, b, (((1,), (1,)), ((), ())), preferred_element_type=F32)


def _mod_kernel(c_ref, w_ref, b_ref, o_ref):
    c = c_ref[...]
    s = c * _sigmoid(c)
    o_ref[0] = _dot(s.astype(BF16), w_ref[0].astype(BF16)) + b_ref[0]


def _modulation(c_rows, w_mod, b_mod):
    depth, d, n = w_mod.shape
    rows = c_rows.shape[0]
    tn = 1536
    return pl.pallas_call(
        _mod_kernel,
        out_shape=jax.ShapeDtypeStruct((depth, rows, n), F32),
        grid=(depth, n // tn),
        in_specs=[
            pl.BlockSpec((rows, d), lambda l, j: (0, 0)),
            pl.BlockSpec((1, d, tn), lambda l, j: (l, 0, j)),
            pl.BlockSpec((1, 1, tn), lambda l, j: (l, 0, j)),
        ],
        out_specs=pl.BlockSpec((1, rows, tn), lambda l, j: (l, 0, j)),
        compiler_params=_params(("arbitrary", "arbitrary"), 40),
        name="modulation",
    )(c_rows, w_mod, b_mod.reshape(depth, 1, n))


def _rope128(v, cos, sin_signed, first_half):
    partner = jnp.where(first_half, pltpu.roll(v, LANES - 16, axis=1), pltpu.roll(v, 16, axis=1))
    return v * cos + partner * sin_signed


def _in_proj_kernel(x_ref, g_ref, sc_ref, sh_ref, w_ref, cos_ref, sin_ref, o_ref, h_scr, *, rope, tn):
    j = pl.program_id(1)

    @pl.when(j == 0)
    def _():
        x = x_ref[...]
        ms = jnp.mean(x * x, axis=-1, keepdims=True)
        h = x * lax.rsqrt(ms + EPS) * g_ref[...]
        h = h * (1.0 + sc_ref[0]) + sh_ref[0]
        h_scr[...] = h.astype(BF16)

    acc = _dot(h_scr[...], w_ref[...])
    if rope:
        jq = COL_QA // tn
        jk = COL_KA // tn
        is_rot = jnp.logical_or(j == jq, j == jk)

        @pl.when(is_rot)
        def _():
            cos = cos_ref[...]
            sin = sin_ref[...]
            lane = lax.broadcasted_iota(jnp.int32, (1, LANES), 1)
            first_half = (lane % 32) < 16
            pieces = [
                _rope128(acc[:, k * LANES:(k + 1) * LANES], cos, sin, first_half)
                for k in range(tn // LANES)
            ]
            o_ref[...] = jnp.concatenate(pieces, axis=1).astype(BF16)

        @pl.when(jnp.logical_not(is_rot))
        def _():
            o_ref[...] = acc.astype(BF16)
    else:
        o_ref[...] = acc.astype(BF16)


def _in_proj(x_flat, g, sc, sh, w_bf16, cos_t, sin_t, *, rows_per_mod, seq, rope, tm):
    t, d = x_flat.shape
    n = w_bf16.shape[1]
    tn = 512
    bpm = rows_per_mod // tm
    bps = seq // tm
    kern = functools.partial(_in_proj_kernel, rope=rope, tn=tn)
    return pl.pallas_call(
        kern,
        out_shape=jax.ShapeDtypeStruct((t, n), BF16),
        grid=(t // tm, n // tn),
        in_specs=[
            pl.BlockSpec((tm, d), lambda i, j: (i, 0)),
            pl.BlockSpec((1, d), lambda i, j: (0, 0)),
            pl.BlockSpec((1, 1, d), lambda i, j: (i // bpm, 0, 0)),
            pl.BlockSpec((1, 1, d), lambda i, j: (i // bpm, 0, 0)),
            pl.BlockSpec((d, tn), lambda i, j: (0, j)),
            pl.BlockSpec((tm, LANES), lambda i, j: (i % bps, 0)),
            pl.BlockSpec((tm, LANES), lambda i, j: (i % bps, 0)),
        ],
        out_specs=pl.BlockSpec((tm, tn), lambda i, j: (i, j)),
        scratch_shapes=[pltpu.VMEM((tm, d), BF16)],
        compiler_params=_params(("arbitrary", "arbitrary"), 48),
        name="in_proj_rope" if rope else "in_proj",
    )(x_flat, g.reshape(1, d), sc, sh, w_bf16, cos_t, sin_t)


def _diff_attn_kernel(*refs, lam_init, n_kv):
    q_ref = refs[0]
    k_refs = refs[1:1 + n_kv]
    v_refs = refs[1 + n_kv:1 + 2 * n_kv]
    lam_ref, g_ref, o_ref, vext_scr = refs[1 + 2 * n_kv:]
    sizes = [v_ref.shape[1] for v_ref in v_refs]
    offs = [sum(sizes[:n]) for n in range(n_kv)]

    @pl.when(pl.program_id(2) == 0)
    def _():
        for v_ref, off, n in zip(v_refs, offs, sizes):
            vext_scr[off:off + n, :LANES] = v_ref[0]
            vext_scr[off:off + n, LANES:] = jnp.ones((n, LANES), BF16)

    q = q_ref[0] * (DA_HEAD_DIM ** -0.5)
    lane = lax.broadcasted_iota(jnp.int32, (1, LANES), 1)
    masks = [jnp.where(lane < DA_HEAD_DIM, 1.0, 0.0).astype(BF16),
             jnp.where(lane >= DA_HEAD_DIM, 1.0, 0.0).astype(BF16)]
    ks = [k_ref[0] for k_ref in k_refs]
    scores = [[_dot_nt(q * m, k) for k in ks] for m in masks]
    maxes = [functools.reduce(jnp.maximum, [jnp.max(p, axis=-1, keepdims=True) for p in parts])
             for parts in scores]
    outs = []
    for parts, mx in zip(scores, maxes):
        oe = None
        for p, off, n in zip(parts, offs, sizes):
            term = _dot(jnp.exp(p - mx).astype(BF16), vext_scr[off:off + n, :])
            oe = term if oe is None else oe + term
        outs.append(oe[:, :LANES] * (1.0 / oe[:, LANES:LANES + 1]))
    lp = lam_ref[...]
    t1 = jnp.sum(lp[0:1] * lp[1:2], axis=-1, keepdims=True)
    t2 = jnp.sum(lp[2:3] * lp[3:4], axis=-1, keepdims=True)
    lam = jnp.exp(t1) - jnp.exp(t2) + lam_init
    o = outs[0] - lam * outs[1]
    ms = jnp.mean(o * o, axis=-1, keepdims=True)
    o = o * lax.rsqrt(ms + EPS) * g_ref[...] * (1.0 - lam_init)
    o_ref[0] = o.astype(BF16)


def _diff_attention(p_q, kv_sources, lam_params, subln_g, lam_init, *, tq):
    b, nq, _ = p_q.shape
    qa0, ka0, va0 = COL_QA // LANES, COL_KA // LANES, COL_VA // LANES
    n_kv = len(kv_sources)
    in_specs = [pl.BlockSpec((1, tq, LANES), lambda bi, h, i: (bi, i, qa0 + h))]
    for src in kv_sources:
        in_specs.append(pl.BlockSpec((1, src.shape[1], LANES), lambda bi, h, i: (bi, 0, ka0 + h)))
    for src in kv_sources:
        in_specs.append(pl.BlockSpec((1, src.shape[1], LANES), lambda bi, h, i: (bi, 0, va0 + h)))
    in_specs.append(pl.BlockSpec((4, DA_HEAD_DIM), lambda bi, h, i: (0, 0)))
    in_specs.append(pl.BlockSpec((1, LANES), lambda bi, h, i: (0, 0)))
    kern = functools.partial(_diff_attn_kernel, lam_init=lam_init, n_kv=n_kv)
    return pl.pallas_call(
        kern,
        out_shape=jax.ShapeDtypeStruct((b, nq, DA_HEADS * LANES), BF16),
        grid=(b, DA_HEADS, nq // tq),
        in_specs=in_specs,
        out_specs=pl.BlockSpec((1, tq, LANES), lambda bi, h, i: (bi, i, h)),
        scratch_shapes=[pltpu.VMEM((sum(src.shape[1] for src in kv_sources), 2 * LANES), BF16)],
        compiler_params=_params(("arbitrary", "arbitrary", "arbitrary"), 48),
        name="diff_attention",
    )(p_q, *kv_sources, *kv_sources, lam_params, subln_g.reshape(1, LANES))


def _head_masks():
    lane = lax.broadcasted_iota(jnp.int32, (1, LANES), 1)
    return [jnp.where((lane // NA_HEAD_DIM) == hh, 1.0, 0.0).astype(BF16) for hh in range(2)]


def _softmax_stage(score_parts_per_head):
    maxes = [functools.reduce(jnp.maximum, [jnp.max(p, axis=-1, keepdims=True) for p in parts])
             for parts in score_parts_per_head]
    exps = [[jnp.exp(p - m) for p in parts] for parts, m in zip(score_parts_per_head, maxes)]
    invs = [1.0 / functools.reduce(lambda a, b: a + b, [jnp.sum(e, axis=-1, keepdims=True) for e in es])
            for es in exps]
    return exps, invs


def _head_pair_attention(q_pair, score_fn, value_fn):
    lane = lax.broadcasted_iota(jnp.int32, (1, LANES), 1)
    masks = _head_masks()
    exps, invs = _softmax_stage([score_fn(q_pair * masks[hh], hh) for hh in range(2)])
    outs = [value_fn([e.astype(BF16) for e in es]) * inv for es, inv in zip(exps, invs)]
    return jnp.where(lane < NA_HEAD_DIM, outs[0], outs[1])


def _nbr_attn_kernel(q_ref, kx_ref, vx_ref, kc_ref, vc_ref, bias_ref, o_ref, *, rows):
    r = pl.program_id(1)
    scale = NA_HEAD_DIM ** -0.5
    rs = jnp.clip(r - NA_KH // 2, 0, rows - NA_KH)
    dr0 = rs - r + NA_KH - 1
    win = NA_KH * GRID_W
    start = pl.multiple_of(rs * GRID_W, GRID_W)
    lane = lax.broadcasted_iota(jnp.int32, (1, LANES), 1)
    masks = _head_masks()
    n_pairs = NA_HEADS // 2
    cols = [slice(p * LANES, (p + 1) * LANES) for p in range(n_pairs)]
    scores = []
    for p in range(n_pairs):
        q_pair = q_ref[0, :, cols[p]] * scale
        k_win = kx_ref[0, pl.ds(start, win), cols[p]]
        k_ctx = kc_ref[0, :, cols[p]]
        for hh in range(2):
            qm = q_pair * masks[hh]
            bias = jnp.concatenate(
                [bias_ref[2 * p + hh, dr0 + 2 * jj] for jj in range(NA_KH // 2)], axis=1)
            scores.append([_dot_nt(qm, k_win) + bias, _dot_nt(qm, k_ctx)])
    exps, invs = _softmax_stage(scores)
    for p in range(n_pairs):
        v_win = vx_ref[0, pl.ds(start, win), cols[p]]
        v_ctx = vc_ref[0, :, cols[p]]
        outs = []
        for hh in range(2):
            e_win, e_ctx = exps[2 * p + hh]
            o = _dot(e_win.astype(BF16), v_win) + _dot(e_ctx.astype(BF16), v_ctx)
            outs.append(o * invs[2 * p + hh])
        o_ref[0, :, cols[p]] = jnp.where(lane < NA_HEAD_DIM, outs[0], outs[1]).astype(BF16)


def _nbr_attention(p_x, p_y, bias_t2):
    b, s, _ = p_x.shape
    c = p_y.shape[1]
    rows = s // GRID_W
    w = NA_HEADS * NA_HEAD_DIM
    jq, jk, jv = COL_QN // w, COL_KN // w, COL_VN // w
    kern = functools.partial(_nbr_attn_kernel, rows=rows)
    return pl.pallas_call(
        kern,
        out_shape=jax.ShapeDtypeStruct((b, s, w), BF16),
        grid=(b, rows),
        in_specs=[
            pl.BlockSpec((1, GRID_W, w), lambda bi, r: (bi, r, jq)),
            pl.BlockSpec((1, s, w), lambda bi, r: (bi, 0, jk)),
            pl.BlockSpec((1, s, w), lambda bi, r: (bi, 0, jv)),
            pl.BlockSpec((1, c, w), lambda bi, r: (bi, 0, jk)),
            pl.BlockSpec((1, c, w), lambda bi, r: (bi, 0, jv)),
            pl.BlockSpec(bias_t2.shape, lambda bi, r: (0, 0, 0, 0)),
        ],
        out_specs=pl.BlockSpec((1, GRID_W, w), lambda bi, r: (bi, r, 0)),
        compiler_params=_params(("arbitrary", "arbitrary"), 48),
        name="nbr_attention",
    )(p_x, p_x, p_x, p_y, p_y, bias_t2)


def _ctx_attn_kernel(q_ref, k_ref, v_ref, o_ref):
    scale = NA_HEAD_DIM ** -0.5
    for p in range(NA_HEADS // 2):
        cols = slice(p * LANES, (p + 1) * LANES)
        k = k_ref[0, :, cols]
        v = v_ref[0, :, cols]
        o = _head_pair_attention(
            q_ref[0, :, cols],
            lambda qm, hh, k=k: [_dot_nt(qm, k) * scale],
            lambda probs, v=v: _dot(probs[0], v))
        o_ref[0, :, cols] = o.astype(BF16)


def _ctx_attention(p_y):
    b, c, _ = p_y.shape
    w = NA_HEADS * NA_HEAD_DIM
    jq, jk, jv = COL_QN // w, COL_KN // w, COL_VN // w
    return pl.pallas_call(
        _ctx_attn_kernel,
        out_shape=jax.ShapeDtypeStruct((b, c, w), BF16),
        grid=(b,),
        in_specs=[
            pl.BlockSpec((1, c, w), lambda bi: (bi, 0, jq)),
            pl.BlockSpec((1, c, w), lambda bi: (bi, 0, jk)),
            pl.BlockSpec((1, c, w), lambda bi: (bi, 0, jv)),
        ],
        out_specs=pl.BlockSpec((1, c, w), lambda bi: (bi, 0, 0)),
        compiler_params=_params(("arbitrary",), 32),
        name="ctx_attention",
    )(p_y, p_y, p_y)


def _fourier_kernel(f_ref, cc_ref, sc_ref, cn_ref, sn_ref, o_ref, ab_scr, *, norm):
    i = pl.program_id(1)
    w = FT_GROUPS * FT_GROUP_DIM

    @pl.when(i == 0)
    def _():
        for g in range(FT_GROUPS):
            u = f_ref[0, :, g * FT_GROUP_DIM:(g + 1) * FT_GROUP_DIM]
            ab_scr[:, g * FT_GROUP_DIM:(g + 1) * FT_GROUP_DIM] = _dot(u, cc_ref[...]).astype(BF16)
            ab_scr[:, w + g * FT_GROUP_DIM:w + (g + 1) * FT_GROUP_DIM] = _dot(u, sc_ref[...]).astype(BF16)

    o = _dot(cn_ref[...], ab_scr[:, :w]) - _dot(sn_ref[...], ab_scr[:, w:])
    o_ref[0] = (o * norm).astype(BF16)


def _dft_tables(n):
    j = jnp.arange(n, dtype=jnp.int32)
    m = (j[:, None] * j[None, :]) % n
    ang = m.astype(F32) * (2.0 * math.pi / n)
    return jnp.cos(ang).astype(BF16), jnp.sin(ang).astype(BF16)


def _fourier_mix(p, tables_n, tables_c, *, tm):
    b, n, _ = p.shape
    w = FT_GROUPS * FT_GROUP_DIM
    cn, sn = tables_n
    cc, sc = tables_c
    jf = COL_F // w
    kern = functools.partial(_fourier_kernel, norm=1.0 / math.sqrt(n * FT_GROUP_DIM))
    return pl.pallas_call(
        kern,
        out_shape=jax.ShapeDtypeStruct((b, n, w), BF16),
        grid=(b, n // tm),
        in_specs=[
            pl.BlockSpec((1, n, w), lambda bi, i: (bi, 0, jf)),
            pl.BlockSpec(cc.shape, lambda bi, i: (0, 0)),
            pl.BlockSpec(sc.shape, lambda bi, i: (0, 0)),
            pl.BlockSpec((tm, n), lambda bi, i: (i, 0)),
            pl.BlockSpec((tm, n), lambda bi, i: (i, 0)),
        ],
        out_specs=pl.BlockSpec((1, tm, w), lambda bi, i: (bi, i, 0)),
        scratch_shapes=[pltpu.VMEM((n, 2 * w), BF16)],
        compiler_params=_params(("arbitrary", "arbitrary"), 48),
        name="fourier_mix",
    )(p, cc, sc, cn, sn)


def _pack_bf16_pair(lo, hi):
    ulo = lax.bitcast_convert_type(lo.astype(BF16).astype(F32), jnp.uint32)
    uhi = lax.bitcast_convert_type(hi.astype(BF16).astype(F32), jnp.uint32)
    return uhi | (ulo >> 16)


def _unpack_bf16_pair(w):
    lo = lax.bitcast_convert_type(w << 16, F32)
    hi = lax.bitcast_convert_type(w & jnp.uint32(0xFFFF0000), F32)
    return lo, hi


def _merge_kernel(oa_ref, ob_ref, oc_ref, *rest):
    n_gt = 2 * N_BRANCHES
    gt_refs = rest[:n_gt]
    (x_ref, g1_ref, wb_ref, wo_ref, gn_ref, sc_ref, sh_ref, wrh_ref, wrl_ref, br_ref,
     xo_ref, h_ref, ti_ref, tw_ref, cnt_ref) = rest[n_gt:]
    d = x_ref.shape[1]
    tm = x_ref.shape[0]
    half = d // 2

    @pl.when(pl.program_id(0) == 0)
    def _():
        cnt_ref[...] = jnp.zeros(cnt_ref.shape, F32)
    m = [None, None]
    for i, o_ref in enumerate((oa_ref, ob_ref, oc_ref)):
        proj = _dot(o_ref[...], wb_ref[i])
        for hf in range(2):
            term = _sigmoid(gt_refs[2 * i + hf][...].astype(F32)) * proj[:, hf * half:(hf + 1) * half]
            m[hf] = term if m[hf] is None else m[hf] + term
    mix = _dot(jnp.concatenate(m, axis=1).astype(BF16), wo_ref[...])
    x = x_ref[...] + g1_ref[0] * mix
    xo_ref[...] = x
    ms = jnp.mean(x * x, axis=-1, keepdims=True)
    h = x * lax.rsqrt(ms + EPS) * gn_ref[...]
    h = h * (1.0 + sc_ref[0]) + sh_ref[0]
    h_ref[...] = _pack_bf16_pair(h[:, :half], h[:, half:])
    h_hi = h.astype(BF16)
    h_lo = (h - h_hi.astype(F32)).astype(BF16)
    logits = (_dot(h_hi, wrh_ref[...]) + (_dot(h_hi, wrl_ref[...]) + _dot(h_lo, wrh_ref[...]))
              + br_ref[...])
    lane = lax.broadcasted_iota(jnp.int32, logits.shape, 1).astype(F32)
    cur = logits
    vals, idxs = [], []
    for _ in range(TOP_K):
        mx = jnp.max(cur, axis=-1, keepdims=True)
        ix = jnp.min(jnp.where(cur == mx, lane, float(LANES)), axis=-1, keepdims=True)
        vals.append(mx)
        idxs.append(ix)
        cur = jnp.where(lane == ix, -jnp.inf, cur)
    es = [jnp.exp(v - vals[0]) for v in vals]
    inv = 1.0 / functools.reduce(lambda a, b: a + b, es)
    member = functools.reduce(lambda a, b: a + b, [jnp.where(lane == ix, 1.0, 0.0) for ix in idxs])
    row_i = lax.broadcasted_iota(jnp.int32, (tm, tm), 0)
    col_i = lax.broadcasted_iota(jnp.int32, (tm, tm), 1)
    lower = jnp.where(col_i < row_i, 1.0, 0.0).astype(BF16)
    prefix = _dot(lower, member.astype(BF16)) + cnt_ref[0:1, :]
    ti = jnp.zeros(logits.shape, F32)
    tw = jnp.zeros(logits.shape, F32)
    for k in range(TOP_K):
        rank = jnp.sum(jnp.where(lane == idxs[k], prefix, 0.0), axis=-1, keepdims=True)
        ti = jnp.where(lane == float(k), idxs[k], ti)
        ti = jnp.where(lane == float(TOP_K + k), rank, ti)
        tw = jnp.where(lane == float(k), es[k] * inv, tw)
    ti_ref[...] = ti.astype(jnp.int32)
    tw_ref[...] = tw
    cnt_ref[...] = cnt_ref[...] + jnp.sum(member, axis=0, keepdims=True)


def _merge(oa, ob, oc, p_flat, x_flat, g1, w_branch, w_out, g_ffn, sc2, sh2, wr_hi, wr_lo, b_router,
           *, rows_per_mod, tm):
    t, d = x_flat.shape
    bw = BRANCH_WIDTH
    bpm = rows_per_mod // tm
    row = lambda i: (i, 0)
    mod = lambda i: (i // bpm, 0, 0)
    whole2 = lambda i: (0, 0)
    n_gt = 2 * N_BRANCHES
    gw = N_BRANCHES * d // n_gt
    gate_specs = [pl.BlockSpec((tm, gw), functools.partial(lambda i, j: (i, j), j=COL_GATE // gw + j))
                  for j in range(n_gt)]
    return pl.pallas_call(
        _merge_kernel,
        out_shape=(
            jax.ShapeDtypeStruct((t, d), F32),
            jax.ShapeDtypeStruct((t, d // 2), jnp.uint32),
            jax.ShapeDtypeStruct((t, LANES), jnp.int32),
            jax.ShapeDtypeStruct((t, LANES), F32),
            jax.ShapeDtypeStruct((8, LANES), F32),
        ),
        grid=(t // tm,),
        in_specs=[
            pl.BlockSpec((tm, bw), row),
            pl.BlockSpec((tm, bw), row),
            pl.BlockSpec((tm, bw), row),
            *gate_specs,
            pl.BlockSpec((tm, d), row),
            pl.BlockSpec((1, 1, d), mod),
            pl.BlockSpec((N_BRANCHES, bw, d), lambda i: (0, 0, 0)),
            pl.BlockSpec((d, d), whole2),
            pl.BlockSpec((1, d), whole2),
            pl.BlockSpec((1, 1, d), mod),
            pl.BlockSpec((1, 1, d), mod),
            pl.BlockSpec((d, LANES), whole2),
            pl.BlockSpec((d, LANES), whole2),
            pl.BlockSpec((1, LANES), whole2),
        ],
        out_specs=(
            pl.BlockSpec((tm, d), row),
            pl.BlockSpec((tm, d // 2), row),
            pl.BlockSpec((tm, LANES), row),
            pl.BlockSpec((tm, LANES), row),
            pl.BlockSpec((8, LANES), whole2),
        ),
        compiler_params=_params(("arbitrary",), 48),
        name="merge_route",
    )(oa, ob, oc, *([p_flat] * n_gt), x_flat, g1, w_branch, w_out, g_ffn.reshape(1, d), sc2, sh2,
      wr_hi, wr_lo, b_router)


def _dispatch_kernel(slot_ref, h_ref, xs_in, xs_out, sem):
    del xs_in
    tm = h_ref.shape[0]

    def issue(r, carry):
        for k in range(TOP_K):
            s = slot_ref[0, 0, r * TOP_K + k]
            pltpu.make_async_copy(h_ref.at[pl.ds(r, 1)], xs_out.at[pl.ds(s, 1)], sem).start(
                priority=k % 2)
        return carry

    lax.fori_loop(0, tm, issue, 0)
    for k in range(TOP_K):
        pltpu.make_async_copy(h_ref, xs_out.at[pl.ds(0, tm)], sem).wait()


def _dispatch(slots, h_packed, n_slots, *, tm):
    t, w = h_packed.shape
    return pl.pallas_call(
        _dispatch_kernel,
        out_shape=jax.ShapeDtypeStruct((n_slots, w), h_packed.dtype),
        grid=(t // tm,),
        in_specs=[
            pl.BlockSpec((1, 1, tm * TOP_K), lambda i: (i, 0, 0), memory_space=pltpu.SMEM),
            pl.BlockSpec((tm, w), lambda i: (i, 0)),
            pl.BlockSpec(memory_space=pl.ANY),
        ],
        out_specs=pl.BlockSpec(memory_space=pl.ANY),
        scratch_shapes=[pltpu.SemaphoreType.DMA(())],
        input_output_aliases={2: 0},
        compiler_params=_params(("arbitrary",), 32),
        name="moe_dispatch",
    )(slots.reshape(t // tm, 1, tm * TOP_K), h_packed, jnp.zeros((n_slots, w), h_packed.dtype))


def _expert_kernel(be_ref, nu_ref, x_ref, wgu_ref, bgu_ref, wd_ref, bd_ref, y_ref, wgu_scr, wd_scr):
    i = pl.program_id(0)
    f = wd_ref.shape[1]
    d = wgu_ref.shape[1]
    new_expert = jnp.logical_or(i == 0, be_ref[i] != be_ref[jnp.maximum(i - 1, 0)])

    @pl.when(new_expert)
    def _():
        chunk = 256
        for c in range(d // chunk):
            wgu_scr[c * chunk:(c + 1) * chunk, :] = wgu_ref[0, c * chunk:(c + 1) * chunk, :].astype(BF16)
        for c in range(f // chunk):
            wd_scr[c * chunk:(c + 1) * chunk, :] = wd_ref[0, c * chunk:(c + 1) * chunk, :].astype(BF16)

    @pl.when(i < nu_ref[0])
    def _():
        lo, hi = _unpack_bf16_pair(x_ref[...])
        x = jnp.concatenate([lo, hi], axis=1).astype(BF16)
        gu = _dot(x, wgu_scr[...]) + bgu_ref[0]
        gate = jnp.minimum(gu[:, :f], SWIGLU_LIMIT)
        up = jnp.clip(gu[:, f:], -SWIGLU_LIMIT, SWIGLU_LIMIT)
        act = gate * _sigmoid(SWIGLU_ALPHA * gate) * (up + 1.0)
        y_ref[...] = _dot(act.astype(BF16), wd_scr[...]) + bd_ref[0]

    @pl.when(i >= nu_ref[0])
    def _():
        y_ref[...] = jnp.zeros(y_ref.shape, y_ref.dtype)


def _expert_ffn(xs, block_e, n_used, wgu, bgu, wd, bd):
    n_slots, w = xs.shape
    n_blocks = block_e.shape[0]
    e, d, f2 = wgu.shape
    f = f2 // 2
    grid_spec = pltpu.PrefetchScalarGridSpec(
        num_scalar_prefetch=2,
        grid=(n_blocks,),
        in_specs=[
            pl.BlockSpec((MOE_BLOCK, w), lambda i, be, nu: (i, 0)),
            pl.BlockSpec((1, d, f2), lambda i, be, nu: (be[i], 0, 0)),
            pl.BlockSpec((1, 1, f2), lambda i, be, nu: (be[i], 0, 0)),
            pl.BlockSpec((1, f, d), lambda i, be, nu: (be[i], 0, 0)),
            pl.BlockSpec((1, 1, d), lambda i, be, nu: (be[i], 0, 0)),
        ],
        out_specs=pl.BlockSpec((MOE_BLOCK, d), lambda i, be, nu: (i, 0)),
        scratch_shapes=[pltpu.VMEM((d, f2), BF16), pltpu.VMEM((f, d), BF16)],
    )
    return pl.pallas_call(
        _expert_kernel,
        out_shape=jax.ShapeDtypeStruct((n_slots, d), F32),
        grid_spec=grid_spec,
        compiler_params=_params(("arbitrary",), 56),
        name="expert_ffn",
    )(block_e, n_used, xs, wgu, bgu.reshape(e, 1, f2), wd, bd.reshape(e, 1, d))


def _combine_kernel(slot_ref, y_hbm, tw_ref, x_ref, g2_ref, gf_ref, o_ref, ybuf, sem, *, final):
    tm = x_ref.shape[0]

    def issue(r, carry):
        for k in range(TOP_K):
            s = slot_ref[0, 0, r * TOP_K + k]
            pltpu.make_async_copy(y_hbm.at[pl.ds(s, 1)], ybuf.at[k, pl.ds(r, 1)], sem).start(
                priority=k % 2)
        return carry

    lax.fori_loop(0, tm, issue, 0)
    for k in range(TOP_K):
        pltpu.make_async_copy(y_hbm.at[pl.ds(0, tm)], ybuf.at[k], sem).wait()
    tw = tw_ref[...]
    lane = lax.broadcasted_iota(jnp.int32, tw.shape, 1)
    moe = None
    for k in range(TOP_K):
        wk = jnp.sum(jnp.where(lane == k, tw, 0.0), axis=-1, keepdims=True)
        term = wk * ybuf[k]
        moe = term if moe is None else moe + term
    x = x_ref[...] + g2_ref[0] * moe
    if final:
        ms = jnp.mean(x * x, axis=-1, keepdims=True)
        x = x * lax.rsqrt(ms + EPS) * gf_ref[...]
    o_ref[...] = x


def _combine(slots, y_slots, top_w, x_flat, g2, final_g, *, rows_per_mod, final, tm):
    t, d = x_flat.shape
    bpm = rows_per_mod // tm
    kern = functools.partial(_combine_kernel, final=final)
    return pl.pallas_call(
        kern,
        out_shape=jax.ShapeDtypeStruct((t, d), F32),
        grid=(t // tm,),
        in_specs=[
            pl.BlockSpec((1, 1, tm * TOP_K), lambda i: (i, 0, 0), memory_space=pltpu.SMEM),
            pl.BlockSpec(memory_space=pl.ANY),
            pl.BlockSpec((tm, LANES), lambda i: (i, 0)),
            pl.BlockSpec((tm, d), lambda i: (i, 0)),
            pl.BlockSpec((1, 1, d), lambda i: (i // bpm, 0, 0)),
            pl.BlockSpec((1, d), lambda i: (0, 0)),
        ],
        out_specs=pl.BlockSpec((tm, d), lambda i: (i, 0)),
        scratch_shapes=[pltpu.VMEM((TOP_K, tm, d), F32), pltpu.SemaphoreType.DMA(())],
        compiler_params=_params(("arbitrary",), 32),
        name="moe_combine",
    )(slots.reshape(t // tm, 1, tm * TOP_K), y_slots, top_w, x_flat, g2, final_g.reshape(1, d))


def _route_plan(top_i, counts):
    t = top_i.shape[0]
    experts = jnp.arange(N_EXPERTS, dtype=jnp.int32)
    counts = counts.astype(jnp.int32)
    nblk = (counts + MOE_BLOCK - 1) // MOE_BLOCK
    blk_end = jnp.cumsum(nblk)
    base = (blk_end - nblk) * MOE_BLOCK
    e = top_i[:, :TOP_K]
    rank = top_i[:, TOP_K:2 * TOP_K]
    slot = jnp.sum(jnp.where(e[..., None] == experts, base, 0), axis=-1) + rank
    n_blocks = -(-(t * TOP_K) // MOE_BLOCK) + N_EXPERTS
    blocks = jnp.arange(n_blocks, dtype=jnp.int32)
    block_e = jnp.minimum(jnp.sum((blk_end[None, :] <= blocks[:, None]).astype(jnp.int32), axis=1),
                          N_EXPERTS - 1)
    return slot.reshape(-1), block_e, blk_end[-1:]


def _moe(h_packed, top_i, top_w, counts, x_flat, g2, final_g, wgu, bgu, wd, bd, *,
         rows_per_mod, final):
    slot, block_e, n_used = _route_plan(top_i, counts[0, :N_EXPERTS])
    xs = _dispatch(slot, h_packed, block_e.shape[0] * MOE_BLOCK, tm=min(512, h_packed.shape[0]))
    y_slots = _expert_ffn(xs, block_e, n_used, wgu, bgu, wd, bd)
    return _combine(slot, y_slots, top_w, x_flat, g2, final_g,
                    rows_per_mod=rows_per_mod, final=final, tm=256)


def _rope_tables(n):
    t = jnp.arange(n)
    row = (t // GRID_W).astype(F32)
    col = (t % GRID_W).astype(F32)
    quarter = DA_HEAD_DIM // 4
    freqs = ROPE_BASE ** (-jnp.arange(quarter, dtype=F32) / quarter)
    lane = jnp.arange(LANES)
    pos = jnp.where(((lane % DA_HEAD_DIM) // (DA_HEAD_DIM // 2))[None, :] == 0, row[:, None], col[:, None])
    ang = pos * freqs[lane % quarter][None, :]
    sign = jnp.where((lane % (DA_HEAD_DIM // 2)) < quarter, -1.0, 1.0)[None, :]
    return jnp.cos(ang), jnp.sin(ang) * sign


def _nbr_bias_table(rpb):
    col = jnp.arange(GRID_W)
    col_start = jnp.clip(col - NA_KW // 2, 0, GRID_W - NA_KW)
    mask = (col[None, :] >= col_start[:, None]) & (col[None, :] < col_start[:, None] + NA_KW)
    dc = jnp.clip(col[None, :] - col[:, None], 1 - NA_KW, NA_KW - 1) + NA_KW - 1
    tab = jnp.where(mask[None, None], rpb.astype(F32)[:, :, dc], NEG_BIG)
    return jnp.concatenate([tab[:, :-1], tab[:, 1:]], axis=-1)


def _split_hi_lo(w):
    hi = w.astype(BF16)
    return hi, (w - hi.astype(F32)).astype(BF16)


def kernel(x, c, ctx, c_ctx, w_mod, b_mod, norm_mix_g, norm_ffn_g, w_in, da_lambda, da_subln_g,
           na_rpb, w_branch, w_out, w_router, b_router, w_gate_up, b_gate_up, w_down, b_down,
           final_g):
    b, s, d = x.shape
    cl = ctx.shape[1]
    depth = w_mod.shape[0]
    tx, ty = b * s, b * cl

    n_rows = -(-(b + 1) // 8) * 8
    c_rows = jnp.concatenate([c, c_ctx[None], jnp.zeros((n_rows - b - 1, d), F32)], axis=0)
    mod = _modulation(c_rows, w_mod, b_mod)

    cos_t, sin_t = _rope_tables(s)
    tab_s = _dft_tables(s)
    tab_c = _dft_tables(cl)
    tab_g = _dft_tables(FT_GROUP_DIM)

    xf = x.reshape(tx, d)
    yf = ctx.reshape(ty, d)
    for l in range(depth):
        last = l == depth - 1
        lam_init = 0.8 - 0.6 * math.exp(-0.3 * l)
        mods = mod[l].reshape(n_rows, N_MOD, d)
        mx = [mods[:b, i].reshape(b, 1, d) for i in range(N_MOD)]
        my = [mods[b:b + 1, i].reshape(1, 1, d) for i in range(N_MOD)]
        w_in_l = w_in[l].astype(BF16)
        wb_l = w_branch[l].astype(BF16)
        wo_l = w_out[l].astype(BF16)
        wrh_l, wrl_l = _split_hi_lo(jnp.pad(w_router[l], ((0, 0), (0, LANES - N_EXPERTS))))
        br_l = jnp.pad(b_router[l], (0, LANES - N_EXPERTS), constant_values=NEG_BIG).reshape(1, LANES)
        wgu_l = w_gate_up[l]
        wd_l = w_down[l]

        px = _in_proj(xf, norm_mix_g[l], mx[1], mx[0], w_in_l, cos_t, sin_t,
                      rows_per_mod=s, seq=s, rope=True, tm=1024)
        py = _in_proj(yf, norm_mix_g[l], my[1], my[0], w_in_l, cos_t, sin_t,
                      rows_per_mod=ty, seq=ty, rope=False, tm=min(1024, ty))
        px3 = px.reshape(b, s, IN_COLS)
        py3 = py.reshape(b, cl, IN_COLS)

        oa_x = _diff_attention(px3, [px3, py3], da_lambda[l], da_subln_g[l], lam_init, tq=512)
        ob_x = _nbr_attention(px3, py3, _nbr_bias_table(na_rpb[l]))
        oc_x = _fourier_mix(px3, tab_s, tab_g, tm=512)
        xf, hx, tix, twx, cntx = _merge(
            oa_x.reshape(tx, -1), ob_x.reshape(tx, -1), oc_x.reshape(tx, -1), px, xf, mx[2],
            wb_l, wo_l, norm_ffn_g[l], mx[4], mx[3], wrh_l, wrl_l, br_l, rows_per_mod=s, tm=256)

        if not last:
            oa_y = _diff_attention(py3, [py3], da_lambda[l], da_subln_g[l], lam_init, tq=cl)
            ob_y = _ctx_attention(py3)
            oc_y = _fourier_mix(py3, tab_c, tab_g, tm=cl)
            yf, hy, tiy, twy, cnty = _merge(
                oa_y.reshape(ty, -1), ob_y.reshape(ty, -1), oc_y.reshape(ty, -1), py, yf, my[2],
                wb_l, wo_l, norm_ffn_g[l], my[4], my[3], wrh_l, wrl_l, br_l, rows_per_mod=ty, tm=256)
            yf = _moe(hy, tiy, twy, cnty, yf, my[5], final_g, wgu_l, b_gate_up[l], wd_l, b_down[l],
                      rows_per_mod=ty, final=False)

        xf = _moe(hx, tix, twx, cntx, xf, mx[5], final_g, wgu_l, b_gate_up[l], wd_l, b_down[l],
                  rows_per_mod=s, final=last)
    return xf.reshape(b, s, d)
```

```python
import functools
import math

import jax
import jax.numpy as jnp
from jax import lax
from jax.experimental import pallas as pl
from jax.experimental.pallas import tpu as pltpu

F32 = jnp.float32
BF16 = jnp.bfloat16

EPS = 1e-6
GRID_W = 64
N_MOD = 6
DA_HEADS = 4
DA_HEAD_DIM = 64
ROPE_BASE = 10000.0
NA_HEADS = 8
NA_HEAD_DIM = 64
NA_KH = 8
NA_KW = 16
FT_GROUPS = 4
FT_GROUP_DIM = 128
BRANCH_WIDTH = 512
N_BRANCHES = 3
N_EXPERTS = 32
TOP_K = 4
SWIGLU_LIMIT = 7.0
SWIGLU_ALPHA = 1.702

LANES = 128
MOE_BLOCK = 512
ROW_TILE = 8
NEG_BIG = -1e30
MIB = 1024 * 1024

COL_QA = 0
COL_KA = 512
COL_VA = 1024
COL_QN = 1536
COL_KN = 2048
COL_VN = 2560
COL_F = 3072
COL_GATE = 3584
IN_COLS = 6656


def _params(sem, vmem_mib):
    return pltpu.CompilerParams(dimension_semantics=sem, vmem_limit_bytes=vmem_mib * MIB)


def _sigmoid(v):
    return 0.5 * jnp.tanh(0.5 * v) + 0.5


def _dot(a, b):
    return jnp.dot(a, b, preferred_element_type=F32)


def _dot_nt(a, b):
    return lax.dot_general(a, b, (((1,), (1,)), ((), ())), preferred_element_type=F32)


def _mod_kernel(c_ref, w_ref, b_ref, o_ref):
    c = c_ref[...]
    s = c * _sigmoid(c)
    o_ref[0] = _dot(s.astype(BF16), w_ref[0].astype(BF16)) + b_ref[0]


def _modulation(c_rows, w_mod, b_mod):
    depth, d, n = w_mod.shape
    rows = c_rows.shape[0]
    tn = 1536
    return pl.pallas_call(
        _mod_kernel,
        out_shape=jax.ShapeDtypeStruct((depth, rows, n), F32),
        grid=(depth, n // tn),
        in_specs=[
            pl.BlockSpec((rows, d), lambda l, j: (0, 0)),
            pl.BlockSpec((1, d, tn), lambda l, j: (l, 0, j)),
            pl.BlockSpec((1, 1, tn), lambda l, j: (l, 0, j)),
        ],
        out_specs=pl.BlockSpec((1, rows, tn), lambda l, j: (l, 0, j)),
        compiler_params=_params(("arbitrary", "arbitrary"), 40),
        name="modulation",
    )(c_rows, w_mod, b_mod.reshape(depth, 1, n))


def _rope128(v, cos, sin_signed, first_half):
    partner = jnp.where(first_half, pltpu.roll(v, LANES - 16, axis=1), pltpu.roll(v, 16, axis=1))
    return v * cos + partner * sin_signed


def _in_proj_kernel(x_ref, g_ref, sc_ref, sh_ref, w_ref, cos_ref, sin_ref, o_ref, h_scr, *, rope, tn):
    j = pl.program_id(1)

    @pl.when(j == 0)
    def _():
        x = x_ref[...]
        ms = jnp.mean(x * x, axis=-1, keepdims=True)
        h = x * lax.rsqrt(ms + EPS) * g_ref[...]
        h = h * (1.0 + sc_ref[0]) + sh_ref[0]
        h_scr[...] = h.astype(BF16)

    acc = _dot(h_scr[...], w_ref[...])
    if rope:
        jq = COL_QA // tn
        jk = COL_KA // tn
        is_rot = jnp.logical_or(j == jq, j == jk)

        @pl.when(is_rot)
        def _():
            cos = cos_ref[...]
            sin = sin_ref[...]
            lane = lax.broadcasted_iota(jnp.int32, (1, LANES), 1)
            first_half = (lane % 32) < 16
            pieces = [
                _rope128(acc[:, k * LANES:(k + 1) * LANES], cos, sin, first_half)
                for k in range(tn // LANES)
            ]
            o_ref[...] = jnp.concatenate(pieces, axis=1).astype(BF16)

        @pl.when(jnp.logical_not(is_rot))
        def _():
            o_ref[...] = acc.astype(BF16)
    else:
        o_ref[...] = acc.astype(BF16)


def _in_proj(x_flat, g, sc, sh, w_bf16, cos_t, sin_t, *, rows_per_mod, seq, rope, tm):
    t, d = x_flat.shape
    n = w_bf16.shape[1]
    tn = 512
    bpm = rows_per_mod // tm
    bps = seq // tm
    kern = functools.partial(_in_proj_kernel, rope=rope, tn=tn)
    return pl.pallas_call(
        kern,
        out_shape=jax.ShapeDtypeStruct((t, n), BF16),
        grid=(t // tm, n // tn),
        in_specs=[
            pl.BlockSpec((tm, d), lambda i, j: (i, 0)),
            pl.BlockSpec((1, d), lambda i, j: (0, 0)),
            pl.BlockSpec((1, 1, d), lambda i, j: (i // bpm, 0, 0)),
            pl.BlockSpec((1, 1, d), lambda i, j: (i // bpm, 0, 0)),
            pl.BlockSpec((d, tn), lambda i, j: (0, j)),
            pl.BlockSpec((tm, LANES), lambda i, j: (i % bps, 0)),
            pl.BlockSpec((tm, LANES), lambda i, j: (i % bps, 0)),
        ],
        out_specs=pl.BlockSpec((tm, tn), lambda i, j: (i, j)),
        scratch_shapes=[pltpu.VMEM((tm, d), BF16)],
        compiler_params=_params(("arbitrary", "arbitrary"), 48),
        name="in_proj_rope" if rope else "in_proj",
    )(x_flat, g.reshape(1, d), sc, sh, w_bf16, cos_t, sin_t)


def _diff_attn_kernel(*refs, lam_init, n_kv):
    q_ref = refs[0]
    k_refs = refs[1:1 + n_kv]
    v_refs = refs[1 + n_kv:1 + 2 * n_kv]
    lam_ref, g_ref, o_ref, vext_scr = refs[1 + 2 * n_kv:]
    sizes = [v_ref.shape[1] for v_ref in v_refs]
    offs = [sum(sizes[:n]) for n in range(n_kv)]

    @pl.when(pl.program_id(2) == 0)
    def _():
        for v_ref, off, n in zip(v_refs, offs, sizes):
            vext_scr[off:off + n, :LANES] = v_ref[0]
            vext_scr[off:off + n, LANES:] = jnp.ones((n, LANES), BF16)

    q = q_ref[0] * (DA_HEAD_DIM ** -0.5)
    lane = lax.broadcasted_iota(jnp.int32, (1, LANES), 1)
    masks = [jnp.where(lane < DA_HEAD_DIM, 1.0, 0.0).astype(BF16),
             jnp.where(lane >= DA_HEAD_DIM, 1.0, 0.0).astype(BF16)]
    ks = [k_ref[0] for k_ref in k_refs]
    scores = [[_dot_nt(q * m, k) for k in ks] for m in masks]
    maxes = [functools.reduce(jnp.maximum, [jnp.max(p, axis=-1, keepdims=True) for p in parts])
             for parts in scores]
    outs = []
    for parts, mx in zip(scores, maxes):
        oe = None
        for p, off, n in zip(parts, offs, sizes):
            term = _dot(jnp.exp(p - mx).astype(BF16), vext_scr[off:off + n, :])
            oe = term if oe is None else oe + term
        outs.append(oe[:, :LANES] * (1.0 / oe[:, LANES:LANES + 1]))
    lp = lam_ref[...]
    t1 = jnp.sum(lp[0:1] * lp[1:2], axis=-1, keepdims=True)
    t2 = jnp.sum(lp[2:3] * lp[3:4], axis=-1, keepdims=True)
    lam = jnp.exp(t1) - jnp.exp(t2) + lam_init
    o = outs[0] - lam * outs[1]
    ms = jnp.mean(o * o, axis=-1, keepdims=True)
    o = o * lax.rsqrt(ms + EPS) * g_ref[...] * (1.0 - lam_init)
    o_ref[0] = o.astype(BF16)


def _diff_attention(p_q, kv_sources, lam_params, subln_g, lam_init, *, tq):
    b, nq, _ = p_q.shape
    qa0, ka0, va0 = COL_QA // LANES, COL_KA // LANES, COL_VA // LANES
    n_kv = len(kv_sources)
    in_specs = [pl.BlockSpec((1, tq, LANES), lambda bi, h, i: (bi, i, qa0 + h))]
    for src in kv_sources:
        in_specs.append(pl.BlockSpec((1, src.shape[1], LANES), lambda bi, h, i: (bi, 0, ka0 + h)))
    for src in kv_sources:
        in_specs.append(pl.BlockSpec((1, src.shape[1], LANES), lambda bi, h, i: (bi, 0, va0 + h)))
    in_specs.append(pl.BlockSpec((4, DA_HEAD_DIM), lambda bi, h, i: (0, 0)))
    in_specs.append(pl.BlockSpec((1, LANES), lambda bi, h, i: (0, 0)))
    kern = functools.partial(_diff_attn_kernel, lam_init=lam_init, n_kv=n_kv)
    return pl.pallas_call(
        kern,
        out_shape=jax.ShapeDtypeStruct((b, nq, DA_HEADS * LANES), BF16),
        grid=(b, DA_HEADS, nq // tq),
        in_specs=in_specs,
        out_specs=pl.BlockSpec((1, tq, LANES), lambda bi, h, i: (bi, i, h)),
        scratch_shapes=[pltpu.VMEM((sum(src.shape[1] for src in kv_sources), 2 * LANES), BF16)],
        compiler_params=_params(("arbitrary", "arbitrary", "arbitrary"), 48),
        name="diff_attention",
    )(p_q, *kv_sources, *kv_sources, lam_params, subln_g.reshape(1, LANES))


def _head_masks():
    lane = lax.broadcasted_iota(jnp.int32, (1, LANES), 1)
    return [jnp.where((lane // NA_HEAD_DIM) == hh, 1.0, 0.0).astype(BF16) for hh in range(2)]


def _softmax_stage(score_parts_per_head):
    maxes = [functools.reduce(jnp.maximum, [jnp.max(p, axis=-1, keepdims=True) for p in parts])
             for parts in score_parts_per_head]
    exps = [[jnp.exp(p - m) for p in parts] for parts, m in zip(score_parts_per_head, maxes)]
    invs = [1.0 / functools.reduce(lambda a, b: a + b, [jnp.sum(e, axis=-1, keepdims=True) for e in es])
            for es in exps]
    return exps, invs


def _head_pair_attention(q_pair, score_fn, value_fn):
    lane = lax.broadcasted_iota(jnp.int32, (1, LANES), 1)
    masks = _head_masks()
    exps, invs = _softmax_stage([score_fn(q_pair * masks[hh], hh) for hh in range(2)])
    outs = [value_fn([e.astype(BF16) for e in es]) * inv for es, inv in zip(exps, invs)]
    return jnp.where(lane < NA_HEAD_DIM, outs[0], outs[1])


def _nbr_attn_kernel(q_ref, kx_ref, vx_ref, kc_ref, vc_ref, bias_ref, o_ref, *, rows):
    r = pl.program_id(1)
    scale = NA_HEAD_DIM ** -0.5
    rs = jnp.clip(r - NA_KH // 2, 0, rows - NA_KH)
    dr0 = rs - r + NA_KH - 1
    win = NA_KH * GRID_W
    start = pl.multiple_of(rs * GRID_W, GRID_W)
    lane = lax.broadcasted_iota(jnp.int32, (1, LANES), 1)
    masks = _head_masks()
    n_pairs = NA_HEADS // 2
    cols = [slice(p * LANES, (p + 1) * LANES) for p in range(n_pairs)]
    scores = []
    for p in range(n_pairs):
        q_pair = q_ref[0, :, cols[p]] * scale
        k_win = kx_ref[0, pl.ds(start, win), cols[p]]
        k_ctx = kc_ref[0, :, cols[p]]
        for hh in range(2):
            qm = q_pair * masks[hh]
            bias = jnp.concatenate(
                [bias_ref[2 * p + hh, dr0 + 2 * jj] for jj in range(NA_KH // 2)], axis=1)
            scores.append([_dot_nt(qm, k_win) + bias, _dot_nt(qm, k_ctx)])
    exps, invs = _softmax_stage(scores)
    for p in range(n_pairs):
        v_win = vx_ref[0, pl.ds(start, win), cols[p]]
        v_ctx = vc_ref[0, :, cols[p]]
        outs = []
        for hh in range(2):
            e_win, e_ctx = exps[2 * p + hh]
            o = _dot(e_win.astype(BF16), v_win) + _dot(e_ctx.astype(BF16), v_ctx)
            outs.append(o * invs[2 * p + hh])
        o_ref[0, :, cols[p]] = jnp.where(lane < NA_HEAD_DIM, outs[0], outs[1]).astype(BF16)


def _nbr_attention(p_x, p_y, bias_t2):
    b, s, _ = p_x.shape
    c = p_y.shape[1]
    rows = s // GRID_W
    w = NA_HEADS * NA_HEAD_DIM
    jq, jk, jv = COL_QN // w, COL_KN // w, COL_VN // w
    kern = functools.partial(_nbr_attn_kernel, rows=rows)
    return pl.pallas_call(
        kern,
        out_shape=jax.ShapeDtypeStruct((b, s, w), BF16),
        grid=(b, rows),
        in_specs=[
            pl.BlockSpec((1, GRID_W, w), lambda bi, r: (bi, r, jq)),
            pl.BlockSpec((1, s, w), lambda bi, r: (bi, 0, jk)),
            pl.BlockSpec((1, s, w), lambda bi, r: (bi, 0, jv)),
            pl.BlockSpec((1, c, w), lambda bi, r: (bi, 0, jk)),
            pl.BlockSpec((1, c, w), lambda bi, r: (bi, 0, jv)),
            pl.BlockSpec(bias_t2.shape, lambda bi, r: (0, 0, 0, 0)),
        ],
        out_specs=pl.BlockSpec((1, GRID_W, w), lambda bi, r: (bi, r, 0)),
        compiler_params=_params(("arbitrary", "arbitrary"), 48),
        name="nbr_attention",
    )(p_x, p_x, p_x, p_y, p_y, bias_t2)


def _ctx_attn_kernel(q_ref, k_ref, v_ref, o_ref):
    scale = NA_HEAD_DIM ** -0.5
    for p in range(NA_HEADS // 2):
        cols = slice(p * LANES, (p + 1) * LANES)
        k = k_ref[0, :, cols]
        v = v_ref[0, :, cols]
        o = _head_pair_attention(
            q_ref[0, :, cols],
            lambda qm, hh, k=k: [_dot_nt(qm, k) * scale],
            lambda probs, v=v: _dot(probs[0], v))
        o_ref[0, :, cols] = o.astype(BF16)


def _ctx_attention(p_y):
    b, c, _ = p_y.shape
    w = NA_HEADS * NA_HEAD_DIM
    jq, jk, jv = COL_QN // w, COL_KN // w, COL_VN // w
    return pl.pallas_call(
        _ctx_attn_kernel,
        out_shape=jax.ShapeDtypeStruct((b, c, w), BF16),
        grid=(b,),
        in_specs=[
            pl.BlockSpec((1, c, w), lambda bi: (bi, 0, jq)),
            pl.BlockSpec((1, c, w), lambda bi: (bi, 0, jk)),
            pl.BlockSpec((1, c, w), lambda bi: (bi, 0, jv)),
        ],
        out_specs=pl.BlockSpec((1, c, w), lambda bi: (bi, 0, 0)),
        compiler_params=_params(("arbitrary",), 32),
        name="ctx_attention",
    )(p_y, p_y, p_y)


def _fourier_kernel(f_ref, cc_ref, sc_ref, cn_ref, sn_ref, o_ref, ab_scr, *, norm):
    i = pl.program_id(1)
    w = FT_GROUPS * FT_GROUP_DIM

    @pl.when(i == 0)
    def _():
        for g in range(FT_GROUPS):
            u = f_ref[0, :, g * FT_GROUP_DIM:(g + 1) * FT_GROUP_DIM]
            ab_scr[:, g * FT_GROUP_DIM:(g + 1) * FT_GROUP_DIM] = _dot(u, cc_ref[...]).astype(BF16)
            ab_scr[:, w + g * FT_GROUP_DIM:w + (g + 1) * FT_GROUP_DIM] = _dot(u, sc_ref[...]).astype(BF16)

    o = _dot(cn_ref[...], ab_scr[:, :w]) - _dot(sn_ref[...], ab_scr[:, w:])
    o_ref[0] = (o * norm).astype(BF16)


def _dft_tables(n):
    j = jnp.arange(n, dtype=jnp.int32)
    m = (j[:, None] * j[None, :]) % n
    ang = m.astype(F32) * (2.0 * math.pi / n)
    return jnp.cos(ang).astype(BF16), jnp.sin(ang).astype(BF16)


def _fourier_mix(p, tables_n, tables_c, *, tm):
    b, n, _ = p.shape
    w = FT_GROUPS * FT_GROUP_DIM
    cn, sn = tables_n
    cc, sc = tables_c
    jf = COL_F // w
    kern = functools.partial(_fourier_kernel, norm=1.0 / math.sqrt(n * FT_GROUP_DIM))
    return pl.pallas_call(
        kern,
        out_shape=jax.ShapeDtypeStruct((b, n, w), BF16),
        grid=(b, n // tm),
        in_specs=[
            pl.BlockSpec((1, n, w), lambda bi, i: (bi, 0, jf)),
            pl.BlockSpec(cc.shape, lambda bi, i: (0, 0)),
            pl.BlockSpec(sc.shape, lambda bi, i: (0, 0)),
            pl.BlockSpec((tm, n), lambda bi, i: (i, 0)),
            pl.BlockSpec((tm, n), lambda bi, i: (i, 0)),
        ],
        out_specs=pl.BlockSpec((1, tm, w), lambda bi, i: (bi, i, 0)),
        scratch_shapes=[pltpu.VMEM((n, 2 * w), BF16)],
        compiler_params=_params(("arbitrary", "arbitrary"), 48),
        name="fourier_mix",
    )(p, cc, sc, cn, sn)


def _store_token_tiles(ref, value):
    n, d = value.shape
    chunks = d // LANES
    for c in range(chunks):
        ref[pl.ds(c, n, stride=chunks), :] = value[:, c * LANES:(c + 1) * LANES]


def _load_token_tiles(ref, n, chunks):
    return [ref[pl.ds(c, n, stride=chunks), :] for c in range(chunks)]


def _merge_kernel(oa_ref, ob_ref, oc_ref, *rest):
    n_gt = 2 * N_BRANCHES
    gt_refs = rest[:n_gt]
    (x_ref, g1_ref, wb_ref, wo_ref, gn_ref, sc_ref, sh_ref, wrh_ref, wrl_ref, br_ref, c0_ref,
     xo_ref, h_ref, ti_ref, tw_ref, cnt_ref) = rest[n_gt:]
    d = x_ref.shape[1]
    tm = x_ref.shape[0]
    half = d // 2

    @pl.when(pl.program_id(0) == 0)
    def _():
        cnt_ref[...] = c0_ref[...]
    m = [None, None]
    for i, o_ref in enumerate((oa_ref, ob_ref, oc_ref)):
        proj = _dot(o_ref[...], wb_ref[i])
        for hf in range(2):
            term = _sigmoid(gt_refs[2 * i + hf][...].astype(F32)) * proj[:, hf * half:(hf + 1) * half]
            m[hf] = term if m[hf] is None else m[hf] + term
    mix = _dot(jnp.concatenate(m, axis=1).astype(BF16), wo_ref[...])
    x = x_ref[...] + g1_ref[0] * mix
    xo_ref[...] = x
    ms = jnp.mean(x * x, axis=-1, keepdims=True)
    h = x * lax.rsqrt(ms + EPS) * gn_ref[...]
    h = h * (1.0 + sc_ref[0]) + sh_ref[0]
    _store_token_tiles(h_ref, h)
    h_hi = h.astype(BF16)
    h_lo = (h - h_hi.astype(F32)).astype(BF16)
    logits = (_dot(h_hi, wrh_ref[...]) + (_dot(h_hi, wrl_ref[...]) + _dot(h_lo, wrh_ref[...]))
              + br_ref[...])
    lane = lax.broadcasted_iota(jnp.int32, logits.shape, 1).astype(F32)
    cur = logits
    vals, idxs = [], []
    for _ in range(TOP_K):
        mx = jnp.max(cur, axis=-1, keepdims=True)
        ix = jnp.min(jnp.where(cur == mx, lane, float(LANES)), axis=-1, keepdims=True)
        vals.append(mx)
        idxs.append(ix)
        cur = jnp.where(lane == ix, -jnp.inf, cur)
    es = [jnp.exp(v - vals[0]) for v in vals]
    inv = 1.0 / functools.reduce(lambda a, b: a + b, es)
    member = functools.reduce(lambda a, b: a + b, [jnp.where(lane == ix, 1.0, 0.0) for ix in idxs])
    row_i = lax.broadcasted_iota(jnp.int32, (tm, tm), 0)
    col_i = lax.broadcasted_iota(jnp.int32, (tm, tm), 1)
    lower = jnp.where(col_i < row_i, 1.0, 0.0).astype(BF16)
    prefix = _dot(lower, member.astype(BF16)) + cnt_ref[0:1, :]
    ti = jnp.zeros(logits.shape, F32)
    tw = jnp.zeros(logits.shape, F32)
    for k in range(TOP_K):
        rank = jnp.sum(jnp.where(lane == idxs[k], prefix, 0.0), axis=-1, keepdims=True)
        ti = jnp.where(lane == float(k), idxs[k], ti)
        ti = jnp.where(lane == float(TOP_K + k), rank, ti)
        tw = jnp.where(lane == float(k), es[k] * inv, tw)
    ti_ref[...] = ti.astype(jnp.int32)
    tw_ref[...] = tw
    cnt_ref[...] = cnt_ref[...] + jnp.sum(member, axis=0, keepdims=True)


def _merge(oa, ob, oc, p_flat, x_flat, g1, w_branch, w_out, g_ffn, sc2, sh2, wr_hi, wr_lo, b_router,
           counts0, *, rows_per_mod, tm):
    t, d = x_flat.shape
    chunks = d // LANES
    bw = BRANCH_WIDTH
    bpm = rows_per_mod // tm
    row = lambda i: (i, 0)
    mod = lambda i: (i // bpm, 0, 0)
    whole2 = lambda i: (0, 0)
    n_gt = 2 * N_BRANCHES
    gw = N_BRANCHES * d // n_gt
    gate_specs = [pl.BlockSpec((tm, gw), functools.partial(lambda i, j: (i, j), j=COL_GATE // gw + j))
                  for j in range(n_gt)]
    return pl.pallas_call(
        _merge_kernel,
        out_shape=(
            jax.ShapeDtypeStruct((t, d), F32),
            jax.ShapeDtypeStruct((t * chunks, LANES), F32),
            jax.ShapeDtypeStruct((t, LANES), jnp.int32),
            jax.ShapeDtypeStruct((t, LANES), F32),
            jax.ShapeDtypeStruct((8, LANES), F32),
        ),
        grid=(t // tm,),
        in_specs=[
            pl.BlockSpec((tm, bw), row),
            pl.BlockSpec((tm, bw), row),
            pl.BlockSpec((tm, bw), row),
            *gate_specs,
            pl.BlockSpec((tm, d), row),
            pl.BlockSpec((1, 1, d), mod),
            pl.BlockSpec((N_BRANCHES, bw, d), lambda i: (0, 0, 0)),
            pl.BlockSpec((d, d), whole2),
            pl.BlockSpec((1, d), whole2),
            pl.BlockSpec((1, 1, d), mod),
            pl.BlockSpec((1, 1, d), mod),
            pl.BlockSpec((d, LANES), whole2),
            pl.BlockSpec((d, LANES), whole2),
            pl.BlockSpec((1, LANES), whole2),
            pl.BlockSpec((8, LANES), whole2),
        ],
        out_specs=(
            pl.BlockSpec((tm, d), row),
            pl.BlockSpec((tm * chunks, LANES), row),
            pl.BlockSpec((tm, LANES), row),
            pl.BlockSpec((tm, LANES), row),
            pl.BlockSpec((8, LANES), whole2),
        ),
        compiler_params=_params(("arbitrary",), 48),
        name="merge_route",
    )(oa, ob, oc, *([p_flat] * n_gt), x_flat, g1, w_branch, w_out, g_ffn.reshape(1, d), sc2, sh2,
      wr_hi, wr_lo, b_router, counts0)


def _dispatch_kernel(slot_ref, h_ref, xs_in, xs_out, sem):
    del xs_in
    rows = h_ref.shape[0]
    tm = rows // ROW_TILE

    def issue(r, carry):
        src = h_ref.at[pl.ds(pl.multiple_of(r * ROW_TILE, ROW_TILE), ROW_TILE)]
        for k in range(TOP_K):
            s = pl.multiple_of(slot_ref[0, 0, r * TOP_K + k], ROW_TILE)
            pltpu.make_async_copy(src, xs_out.at[pl.ds(s, ROW_TILE)], sem).start(priority=k % 2)
        return carry

    lax.fori_loop(0, tm, issue, 0)
    for k in range(TOP_K):
        pltpu.make_async_copy(h_ref, xs_out.at[pl.ds(0, rows)], sem).wait()


def _dispatch(row_starts, h_tiles, xs_init, *, tm):
    t = h_tiles.shape[0] // ROW_TILE
    return pl.pallas_call(
        _dispatch_kernel,
        out_shape=jax.ShapeDtypeStruct(xs_init.shape, xs_init.dtype),
        grid=(t // tm,),
        in_specs=[
            pl.BlockSpec((1, 1, tm * TOP_K), lambda i: (i, 0, 0), memory_space=pltpu.SMEM),
            pl.BlockSpec((tm * ROW_TILE, LANES), lambda i: (i, 0)),
            pl.BlockSpec(memory_space=pl.ANY),
        ],
        out_specs=pl.BlockSpec(memory_space=pl.ANY),
        scratch_shapes=[pltpu.SemaphoreType.DMA(())],
        input_output_aliases={2: 0},
        compiler_params=_params(("arbitrary",), 32),
        name="moe_dispatch",
    )(row_starts.reshape(t // tm, 1, tm * TOP_K), h_tiles, xs_init)


def _expert_kernel(be_ref, nu_ref, x_ref, wgu_ref, bgu_ref, wd_ref, bd_ref, y_ref, wgu_scr, wd_scr):
    i = pl.program_id(0)
    f = wd_ref.shape[2]
    d = wgu_ref.shape[2]
    new_expert = jnp.logical_or(i == 0, be_ref[i] != be_ref[jnp.maximum(i - 1, 0)])

    @pl.when(new_expert)
    def _():
        chunk = 256
        for c in range(d // chunk):
            wgu_scr[c * chunk:(c + 1) * chunk, :] = wgu_ref[0, 0, c * chunk:(c + 1) * chunk, :].astype(BF16)
        for c in range(f // chunk):
            wd_scr[c * chunk:(c + 1) * chunk, :] = wd_ref[0, 0, c * chunk:(c + 1) * chunk, :].astype(BF16)

    @pl.when(i < nu_ref[0])
    def _():
        x = jnp.concatenate(_load_token_tiles(x_ref, MOE_BLOCK, d // LANES), axis=1).astype(BF16)
        gu = _dot(x, wgu_scr[...]) + bgu_ref[0, 0]
        gate = jnp.minimum(gu[:, :f], SWIGLU_LIMIT)
        up = jnp.clip(gu[:, f:], -SWIGLU_LIMIT, SWIGLU_LIMIT)
        act = gate * _sigmoid(SWIGLU_ALPHA * gate) * (up + 1.0)
        _store_token_tiles(y_ref, _dot(act.astype(BF16), wd_scr[...]) + bd_ref[0, 0])

    @pl.when(i >= nu_ref[0])
    def _():
        y_ref[...] = jnp.zeros(y_ref.shape, y_ref.dtype)


def _expert_ffn(xs, block_e, n_used, layer, wgu, bgu, wd, bd):
    n_blocks = block_e.shape[0]
    depth, e, d, f2 = wgu.shape
    f = f2 // 2
    blk_rows = MOE_BLOCK * d // LANES
    grid_spec = pltpu.PrefetchScalarGridSpec(
        num_scalar_prefetch=2,
        grid=(n_blocks,),
        in_specs=[
            pl.BlockSpec((blk_rows, LANES), lambda i, be, nu: (i, 0)),
            pl.BlockSpec((1, 1, d, f2), lambda i, be, nu: (layer, be[i], 0, 0)),
            pl.BlockSpec((1, 1, 1, f2), lambda i, be, nu: (layer, be[i], 0, 0)),
            pl.BlockSpec((1, 1, f, d), lambda i, be, nu: (layer, be[i], 0, 0)),
            pl.BlockSpec((1, 1, 1, d), lambda i, be, nu: (layer, be[i], 0, 0)),
        ],
        out_specs=pl.BlockSpec((blk_rows, LANES), lambda i, be, nu: (i, 0)),
        scratch_shapes=[pltpu.VMEM((d, f2), BF16), pltpu.VMEM((f, d), BF16)],
    )
    return pl.pallas_call(
        _expert_kernel,
        out_shape=jax.ShapeDtypeStruct(xs.shape, F32),
        grid_spec=grid_spec,
        compiler_params=_params(("arbitrary",), 56),
        name="expert_ffn",
    )(block_e, n_used, xs, wgu, bgu.reshape(depth, e, 1, f2), wd, bd.reshape(depth, e, 1, d))


def _combine_kernel(slot_ref, y_hbm, tw_ref, x_ref, g2_ref, gf_ref, o_ref, ybuf, sem, *, final):
    tm = x_ref.shape[0]

    def issue(r, carry):
        dst = pl.ds(pl.multiple_of(r * ROW_TILE, ROW_TILE), ROW_TILE)
        for k in range(TOP_K):
            s = pl.multiple_of(slot_ref[0, 0, r * TOP_K + k], ROW_TILE)
            pltpu.make_async_copy(y_hbm.at[pl.ds(s, ROW_TILE)], ybuf.at[k, dst], sem).start(
                priority=k % 2)
        return carry

    lax.fori_loop(0, tm, issue, 0)
    for k in range(TOP_K):
        pltpu.make_async_copy(y_hbm.at[pl.ds(0, tm * ROW_TILE)], ybuf.at[k], sem).wait()
    tw = tw_ref[...]
    lane = lax.broadcasted_iota(jnp.int32, tw.shape, 1)
    wks = [jnp.sum(jnp.where(lane == k, tw, 0.0), axis=-1, keepdims=True) for k in range(TOP_K)]
    chunks = x_ref.shape[1] // LANES
    tiles = [_load_token_tiles(ybuf.at[k], tm, chunks) for k in range(TOP_K)]
    pieces = []
    for c in range(chunks):
        moe = functools.reduce(lambda a, b: a + b, [wks[k] * tiles[k][c] for k in range(TOP_K)])
        cols = slice(c * LANES, (c + 1) * LANES)
        pieces.append(x_ref[:, cols] + g2_ref[0, :, cols] * moe)
    x = jnp.concatenate(pieces, axis=1)
    if final:
        ms = jnp.mean(x * x, axis=-1, keepdims=True)
        x = x * lax.rsqrt(ms + EPS) * gf_ref[...]
    o_ref[...] = x


def _combine(slots, y_slots, top_w, x_flat, g2, final_g, *, rows_per_mod, final, tm):
    t, d = x_flat.shape
    bpm = rows_per_mod // tm
    kern = functools.partial(_combine_kernel, final=final)
    return pl.pallas_call(
        kern,
        out_shape=jax.ShapeDtypeStruct((t, d), F32),
        grid=(t // tm,),
        in_specs=[
            pl.BlockSpec((1, 1, tm * TOP_K), lambda i: (i, 0, 0), memory_space=pltpu.SMEM),
            pl.BlockSpec(memory_space=pl.ANY),
            pl.BlockSpec((tm, LANES), lambda i: (i, 0)),
            pl.BlockSpec((tm, d), lambda i: (i, 0)),
            pl.BlockSpec((1, 1, d), lambda i: (i // bpm, 0, 0)),
            pl.BlockSpec((1, d), lambda i: (0, 0)),
        ],
        out_specs=pl.BlockSpec((tm, d), lambda i: (i, 0)),
        scratch_shapes=[pltpu.VMEM((TOP_K, tm * d // LANES, LANES), F32), pltpu.SemaphoreType.DMA(())],
        compiler_params=_params(("arbitrary",), 32),
        name="moe_combine",
    )(slots.reshape(t // tm, 1, tm * TOP_K), y_slots, top_w, x_flat, g2, final_g.reshape(1, d))


def _route_plan(counts, n_tokens):
    counts = counts.astype(jnp.int32)
    nblk = (counts + MOE_BLOCK - 1) // MOE_BLOCK
    blk_end = jnp.cumsum(nblk)
    base = (blk_end - nblk) * MOE_BLOCK
    n_blocks = -(-(n_tokens * TOP_K) // MOE_BLOCK) + N_EXPERTS
    blocks = jnp.arange(n_blocks, dtype=jnp.int32)
    block_e = jnp.minimum(jnp.sum((blk_end[None, :] <= blocks[:, None]).astype(jnp.int32), axis=1),
                          N_EXPERTS - 1)
    return base, block_e, blk_end[-1:]


def _slots(top_i, base):
    experts = jnp.arange(N_EXPERTS, dtype=jnp.int32)
    e = top_i[:, :TOP_K]
    rank = top_i[:, TOP_K:2 * TOP_K]
    slot = jnp.sum(jnp.where(e[..., None] == experts, base, 0), axis=-1) + rank
    return (slot * ROW_TILE).reshape(-1)


def _moe(streams, counts, layer, final_g, wgu, bgu, wd, bd, *, final):
    n_tokens = sum(st[3].shape[0] for st in streams)
    base, block_e, n_used = _route_plan(counts[0, :N_EXPERTS], n_tokens)
    xs = jnp.zeros((block_e.shape[0] * MOE_BLOCK * ROW_TILE, LANES), F32)
    slots = [_slots(st[1], base) for st in streams]
    for st, slot in zip(streams, slots):
        xs = _dispatch(slot, st[0], xs, tm=min(512, st[0].shape[0] // ROW_TILE))
    y_slots = _expert_ffn(xs, block_e, n_used, layer, wgu, bgu, wd, bd)
    return [_combine(slot, y_slots, st[2], st[3], st[4], final_g,
                     rows_per_mod=st[5], final=final, tm=256)
            for st, slot in zip(streams, slots)]


def _rope_tables(n):
    t = jnp.arange(n)
    row = (t // GRID_W).astype(F32)
    col = (t % GRID_W).astype(F32)
    quarter = DA_HEAD_DIM // 4
    freqs = ROPE_BASE ** (-jnp.arange(quarter, dtype=F32) / quarter)
    lane = jnp.arange(LANES)
    pos = jnp.where(((lane % DA_HEAD_DIM) // (DA_HEAD_DIM // 2))[None, :] == 0, row[:, None], col[:, None])
    ang = pos * freqs[lane % quarter][None, :]
    sign = jnp.where((lane % (DA_HEAD_DIM // 2)) < quarter, -1.0, 1.0)[None, :]
    return jnp.cos(ang), jnp.sin(ang) * sign


def _nbr_bias_table(rpb):
    col = jnp.arange(GRID_W)
    col_start = jnp.clip(col - NA_KW // 2, 0, GRID_W - NA_KW)
    mask = (col[None, :] >= col_start[:, None]) & (col[None, :] < col_start[:, None] + NA_KW)
    dc = jnp.clip(col[None, :] - col[:, None], 1 - NA_KW, NA_KW - 1) + NA_KW - 1
    tab = jnp.where(mask[None, None], rpb.astype(F32)[:, :, dc], NEG_BIG)
    return jnp.concatenate([tab[:, :-1], tab[:, 1:]], axis=-1)


def _split_hi_lo(w):
    hi = w.astype(BF16)
    return hi, (w - hi.astype(F32)).astype(BF16)


def kernel(x, c, ctx, c_ctx, w_mod, b_mod, norm_mix_g, norm_ffn_g, w_in, da_lambda, da_subln_g,
           na_rpb, w_branch, w_out, w_router, b_router, w_gate_up, b_gate_up, w_down, b_down,
           final_g):
    b, s, d = x.shape
    assert d == ROW_TILE * LANES, "token rows must fill exactly one (8, 128) f32 tile"
    cl = ctx.shape[1]
    depth = w_mod.shape[0]
    tx, ty = b * s, b * cl

    n_rows = -(-(b + 1) // 8) * 8
    c_rows = jnp.concatenate([c, c_ctx[None], jnp.zeros((n_rows - b - 1, d), F32)], axis=0)
    mod = _modulation(c_rows, w_mod, b_mod)

    cos_t, sin_t = _rope_tables(s)
    tab_s = _dft_tables(s)
    tab_c = _dft_tables(cl)
    tab_g = _dft_tables(FT_GROUP_DIM)

    xf = x.reshape(tx, d)
    yf = ctx.reshape(ty, d)
    for l in range(depth):
        last = l == depth - 1
        lam_init = 0.8 - 0.6 * math.exp(-0.3 * l)
        mods = mod[l].reshape(n_rows, N_MOD, d)
        mx = [mods[:b, i].reshape(b, 1, d) for i in range(N_MOD)]
        my = [mods[b:b + 1, i].reshape(1, 1, d) for i in range(N_MOD)]
        w_in_l = w_in[l].astype(BF16)
        wb_l = w_branch[l].astype(BF16)
        wo_l = w_out[l].astype(BF16)
        wrh_l, wrl_l = _split_hi_lo(jnp.pad(w_router[l], ((0, 0), (0, LANES - N_EXPERTS))))
        br_l = jnp.pad(b_router[l], (0, LANES - N_EXPERTS), constant_values=NEG_BIG).reshape(1, LANES)
        zero_counts = jnp.zeros((8, LANES), F32)

        px = _in_proj(xf, norm_mix_g[l], mx[1], mx[0], w_in_l, cos_t, sin_t,
                      rows_per_mod=s, seq=s, rope=True, tm=1024)
        py = _in_proj(yf, norm_mix_g[l], my[1], my[0], w_in_l, cos_t, sin_t,
                      rows_per_mod=ty, seq=ty, rope=False, tm=min(1024, ty))
        px3 = px.reshape(b, s, IN_COLS)
        py3 = py.reshape(b, cl, IN_COLS)

        oa_x = _diff_attention(px3, [px3, py3], da_lambda[l], da_subln_g[l], lam_init, tq=512)
        ob_x = _nbr_attention(px3, py3, _nbr_bias_table(na_rpb[l]))
        oc_x = _fourier_mix(px3, tab_s, tab_g, tm=512)
        streams = []
        counts = zero_counts
        if not last:
            oa_y = _diff_attention(py3, [py3], da_lambda[l], da_subln_g[l], lam_init, tq=cl)
            ob_y = _ctx_attention(py3)
            oc_y = _fourier_mix(py3, tab_c, tab_g, tm=cl)
            yf, hy, tiy, twy, counts = _merge(
                oa_y.reshape(ty, -1), ob_y.reshape(ty, -1), oc_y.reshape(ty, -1), py, yf, my[2],
                wb_l, wo_l, norm_ffn_g[l], my[4], my[3], wrh_l, wrl_l, br_l, counts,
                rows_per_mod=ty, tm=256)
            streams.append((hy, tiy, twy, yf, my[5], ty))
        xf, hx, tix, twx, counts = _merge(
            oa_x.reshape(tx, -1), ob_x.reshape(tx, -1), oc_x.reshape(tx, -1), px, xf, mx[2],
            wb_l, wo_l, norm_ffn_g[l], mx[4], mx[3], wrh_l, wrl_l, br_l, counts,
            rows_per_mod=s, tm=256)
        streams.append((hx, tix, twx, xf, mx[5], s))

        outs = _moe(streams, counts, l, final_g, w_gate_up, b_gate_up, w_down, b_down, final=last)
        xf = outs[-1]
        if not last:
            yf = outs[0]
    return xf.reshape(b, s, d)
```

```python
import functools
import math

import jax
import jax.numpy as jnp
from jax import lax
from jax.experimental import pallas as pl
from jax.experimental.pallas import tpu as pltpu

F32 = jnp.float32
BF16 = jnp.bfloat16

EPS = 1e-6
GRID_W = 64
N_MOD = 6
DA_HEADS = 4
DA_HEAD_DIM = 64
ROPE_BASE = 10000.0
NA_HEADS = 8
NA_HEAD_DIM = 64
NA_KH = 8
NA_KW = 16
FT_GROUPS = 4
FT_GROUP_DIM = 128
BRANCH_WIDTH = 512
N_BRANCHES = 3
N_EXPERTS = 32
TOP_K = 4
SWIGLU_LIMIT = 7.0
SWIGLU_ALPHA = 1.702

LANES = 128
MOE_BLOCK = 512
ROW_TILE = 8
NEG_BIG = -1e30
MIB = 1024 * 1024

COL_QA = 0
COL_KA = 512
COL_VA = 1024
COL_QN = 1536
COL_KN = 2048
COL_VN = 2560
COL_F = 3072
COL_GATE = 3584
IN_COLS = 6656


def _params(sem, vmem_mib):
    return pltpu.CompilerParams(dimension_semantics=sem, vmem_limit_bytes=vmem_mib * MIB)


def _sigmoid(v):
    return 0.5 * jnp.tanh(0.5 * v) + 0.5


def _dot(a, b):
    return jnp.dot(a, b, preferred_element_type=F32)


def _dot_nt(a, b):
    return lax.dot_general(a, b, (((1,), (1,)), ((), ())), preferred_element_type=F32)


def _mod_kernel(c_ref, w_ref, b_ref, o_ref):
    c = c_ref[...]
    s = c * _sigmoid(c)
    o_ref[0] = _dot(s.astype(BF16), w_ref[0].astype(BF16)) + b_ref[0]


def _modulation(c_rows, w_mod, b_mod):
    depth, d, n = w_mod.shape
    rows = c_rows.shape[0]
    tn = 1536
    return pl.pallas_call(
        _mod_kernel,
        out_shape=jax.ShapeDtypeStruct((depth, rows, n), F32),
        grid=(depth, n // tn),
        in_specs=[
            pl.BlockSpec((rows, d), lambda l, j: (0, 0)),
            pl.BlockSpec((1, d, tn), lambda l, j: (l, 0, j)),
            pl.BlockSpec((1, 1, tn), lambda l, j: (l, 0, j)),
        ],
        out_specs=pl.BlockSpec((1, rows, tn), lambda l, j: (l, 0, j)),
        compiler_params=_params(("arbitrary", "arbitrary"), 40),
        name="modulation",
    )(c_rows, w_mod, b_mod.reshape(depth, 1, n))


def _rope128(v, cos, sin_signed, first_half):
    partner = jnp.where(first_half, pltpu.roll(v, LANES - 16, axis=1), pltpu.roll(v, 16, axis=1))
    return v * cos + partner * sin_signed


def _in_proj_kernel(x_ref, g_ref, sc_ref, sh_ref, w_ref, cos_ref, sin_ref, o_ref, h_scr, *, rope, tn):
    j = pl.program_id(1)

    @pl.when(j == 0)
    def _():
        x = x_ref[...]
        ms = jnp.mean(x * x, axis=-1, keepdims=True)
        h = x * lax.rsqrt(ms + EPS) * g_ref[...]
        h = h * (1.0 + sc_ref[0]) + sh_ref[0]
        h_scr[...] = h.astype(BF16)

    acc = _dot(h_scr[...], w_ref[...])
    if rope:
        jq = COL_QA // tn
        jk = COL_KA // tn
        is_rot = jnp.logical_or(j == jq, j == jk)

        @pl.when(is_rot)
        def _():
            cos = cos_ref[...]
            sin = sin_ref[...]
            lane = lax.broadcasted_iota(jnp.int32, (1, LANES), 1)
            first_half = (lane % 32) < 16
            pieces = [
                _rope128(acc[:, k * LANES:(k + 1) * LANES], cos, sin, first_half)
                for k in range(tn // LANES)
            ]
            o_ref[...] = jnp.concatenate(pieces, axis=1).astype(BF16)

        @pl.when(jnp.logical_not(is_rot))
        def _():
            o_ref[...] = acc.astype(BF16)
    else:
        o_ref[...] = acc.astype(BF16)


def _in_proj(x_flat, g, sc, sh, w_bf16, cos_t, sin_t, *, rows_per_mod, seq, rope, tm):
    t, d = x_flat.shape
    n = w_bf16.shape[1]
    tn = 512
    bpm = rows_per_mod // tm
    bps = seq // tm
    kern = functools.partial(_in_proj_kernel, rope=rope, tn=tn)
    return pl.pallas_call(
        kern,
        out_shape=jax.ShapeDtypeStruct((t, n), BF16),
        grid=(t // tm, n // tn),
        in_specs=[
            pl.BlockSpec((tm, d), lambda i, j: (i, 0)),
            pl.BlockSpec((1, d), lambda i, j: (0, 0)),
            pl.BlockSpec((1, 1, d), lambda i, j: (i // bpm, 0, 0)),
            pl.BlockSpec((1, 1, d), lambda i, j: (i // bpm, 0, 0)),
            pl.BlockSpec((d, tn), lambda i, j: (0, j)),
            pl.BlockSpec((tm, LANES), lambda i, j: (i % bps, 0)),
            pl.BlockSpec((tm, LANES), lambda i, j: (i % bps, 0)),
        ],
        out_specs=pl.BlockSpec((tm, tn), lambda i, j: (i, j)),
        scratch_shapes=[pltpu.VMEM((tm, d), BF16)],
        compiler_params=_params(("arbitrary", "arbitrary"), 48),
        name="in_proj_rope" if rope else "in_proj",
    )(x_flat, g.reshape(1, d), sc, sh, w_bf16, cos_t, sin_t)


def _diff_attn_kernel(*refs, lam_init, n_kv):
    q_ref = refs[0]
    k_refs = refs[1:1 + n_kv]
    v_refs = refs[1 + n_kv:1 + 2 * n_kv]
    lam_ref, g_ref, o_ref, vext_scr = refs[1 + 2 * n_kv:]
    sizes = [v_ref.shape[1] for v_ref in v_refs]
    offs = [sum(sizes[:n]) for n in range(n_kv)]

    @pl.when(pl.program_id(2) == 0)
    def _():
        for v_ref, off, n in zip(v_refs, offs, sizes):
            vext_scr[off:off + n, :LANES] = v_ref[0]
            vext_scr[off:off + n, LANES:] = jnp.ones((n, LANES), BF16)

    q = q_ref[0] * (DA_HEAD_DIM ** -0.5)
    lane = lax.broadcasted_iota(jnp.int32, (1, LANES), 1)
    masks = [jnp.where(lane < DA_HEAD_DIM, 1.0, 0.0).astype(BF16),
             jnp.where(lane >= DA_HEAD_DIM, 1.0, 0.0).astype(BF16)]
    ks = [k_ref[0] for k_ref in k_refs]
    scores = [[_dot_nt(q * m, k) for k in ks] for m in masks]
    maxes = [functools.reduce(jnp.maximum, [jnp.max(p, axis=-1, keepdims=True) for p in parts])
             for parts in scores]
    outs = []
    for parts, mx in zip(scores, maxes):
        oe = None
        for p, off, n in zip(parts, offs, sizes):
            term = _dot(jnp.exp(p - mx).astype(BF16), vext_scr[off:off + n, :])
            oe = term if oe is None else oe + term
        outs.append(oe[:, :LANES] * (1.0 / oe[:, LANES:LANES + 1]))
    lp = lam_ref[...]
    t1 = jnp.sum(lp[0:1] * lp[1:2], axis=-1, keepdims=True)
    t2 = jnp.sum(lp[2:3] * lp[3:4], axis=-1, keepdims=True)
    lam = jnp.exp(t1) - jnp.exp(t2) + lam_init
    o = outs[0] - lam * outs[1]
    ms = jnp.mean(o * o, axis=-1, keepdims=True)
    o = o * lax.rsqrt(ms + EPS) * g_ref[...] * (1.0 - lam_init)
    o_ref[0] = o.astype(BF16)


def _diff_attention(p_q, kv_sources, lam_params, subln_g, lam_init, *, tq):
    b, nq, _ = p_q.shape
    qa0, ka0, va0 = COL_QA // LANES, COL_KA // LANES, COL_VA // LANES
    n_kv = len(kv_sources)
    in_specs = [pl.BlockSpec((1, tq, LANES), lambda bi, h, i: (bi, i, qa0 + h))]
    for src in kv_sources:
        in_specs.append(pl.BlockSpec((1, src.shape[1], LANES), lambda bi, h, i: (bi, 0, ka0 + h)))
    for src in kv_sources:
        in_specs.append(pl.BlockSpec((1, src.shape[1], LANES), lambda bi, h, i: (bi, 0, va0 + h)))
    in_specs.append(pl.BlockSpec((4, DA_HEAD_DIM), lambda bi, h, i: (0, 0)))
    in_specs.append(pl.BlockSpec((1, LANES), lambda bi, h, i: (0, 0)))
    kern = functools.partial(_diff_attn_kernel, lam_init=lam_init, n_kv=n_kv)
    return pl.pallas_call(
        kern,
        out_shape=jax.ShapeDtypeStruct((b, nq, DA_HEADS * LANES), BF16),
        grid=(b, DA_HEADS, nq // tq),
        in_specs=in_specs,
        out_specs=pl.BlockSpec((1, tq, LANES), lambda bi, h, i: (bi, i, h)),
        scratch_shapes=[pltpu.VMEM((sum(src.shape[1] for src in kv_sources), 2 * LANES), BF16)],
        compiler_params=_params(("arbitrary", "arbitrary", "arbitrary"), 48),
        name="diff_attention",
    )(p_q, *kv_sources, *kv_sources, lam_params, subln_g.reshape(1, LANES))


def _head_masks():
    lane = lax.broadcasted_iota(jnp.int32, (1, LANES), 1)
    return [jnp.where((lane // NA_HEAD_DIM) == hh, 1.0, 0.0).astype(BF16) for hh in range(2)]


def _softmax_stage(score_parts_per_head):
    maxes = [functools.reduce(jnp.maximum, [jnp.max(p, axis=-1, keepdims=True) for p in parts])
             for parts in score_parts_per_head]
    exps = [[jnp.exp(p - m) for p in parts] for parts, m in zip(score_parts_per_head, maxes)]
    invs = [1.0 / functools.reduce(lambda a, b: a + b, [jnp.sum(e, axis=-1, keepdims=True) for e in es])
            for es in exps]
    return exps, invs


def _head_pair_attention(q_pair, score_fn, value_fn):
    lane = lax.broadcasted_iota(jnp.int32, (1, LANES), 1)
    masks = _head_masks()
    exps, invs = _softmax_stage([score_fn(q_pair * masks[hh], hh) for hh in range(2)])
    outs = [value_fn([e.astype(BF16) for e in es]) * inv for es, inv in zip(exps, invs)]
    return jnp.where(lane < NA_HEAD_DIM, outs[0], outs[1])


def _nbr_attn_kernel(q_ref, kx_ref, vx_ref, kc_ref, vc_ref, bias_ref, o_ref, *, rows):
    r = pl.program_id(1)
    scale = NA_HEAD_DIM ** -0.5
    rs = jnp.clip(r - NA_KH // 2, 0, rows - NA_KH)
    dr0 = rs - r + NA_KH - 1
    win = NA_KH * GRID_W
    start = pl.multiple_of(rs * GRID_W, GRID_W)
    lane = lax.broadcasted_iota(jnp.int32, (1, LANES), 1)
    masks = _head_masks()
    n_pairs = NA_HEADS // 2
    cols = [slice(p * LANES, (p + 1) * LANES) for p in range(n_pairs)]
    nb = q_ref.shape[0]
    scores = []
    for bb in range(nb):
        for p in range(n_pairs):
            q_pair = q_ref[bb, :, cols[p]] * scale
            k_win = kx_ref[bb, pl.ds(start, win), cols[p]]
            k_ctx = kc_ref[bb, :, cols[p]]
            for hh in range(2):
                qm = q_pair * masks[hh]
                bias = jnp.concatenate(
                    [bias_ref[2 * p + hh, dr0 + 2 * jj] for jj in range(NA_KH // 2)], axis=1)
                scores.append([_dot_nt(qm, k_win) + bias, _dot_nt(qm, k_ctx)])
    exps, invs = _softmax_stage(scores)
    for bb in range(nb):
        for p in range(n_pairs):
            v_win = vx_ref[bb, pl.ds(start, win), cols[p]]
            v_ctx = vc_ref[bb, :, cols[p]]
            outs = []
            for hh in range(2):
                idx = (bb * n_pairs + p) * 2 + hh
                e_win, e_ctx = exps[idx]
                o = _dot(e_win.astype(BF16), v_win) + _dot(e_ctx.astype(BF16), v_ctx)
                outs.append(o * invs[idx])
            o_ref[bb, :, cols[p]] = jnp.where(lane < NA_HEAD_DIM, outs[0], outs[1]).astype(BF16)


def _nbr_attention(p_x, p_y, bias_t2):
    b, s, _ = p_x.shape
    c = p_y.shape[1]
    rows = s // GRID_W
    w = NA_HEADS * NA_HEAD_DIM
    jq, jk, jv = COL_QN // w, COL_KN // w, COL_VN // w
    kern = functools.partial(_nbr_attn_kernel, rows=rows)
    nb = 2 if b % 2 == 0 else 1
    return pl.pallas_call(
        kern,
        out_shape=jax.ShapeDtypeStruct((b, s, w), BF16),
        grid=(b // nb, rows),
        in_specs=[
            pl.BlockSpec((nb, GRID_W, w), lambda bi, r: (bi, r, jq)),
            pl.BlockSpec((nb, s, w), lambda bi, r: (bi, 0, jk)),
            pl.BlockSpec((nb, s, w), lambda bi, r: (bi, 0, jv)),
            pl.BlockSpec((nb, c, w), lambda bi, r: (bi, 0, jk)),
            pl.BlockSpec((nb, c, w), lambda bi, r: (bi, 0, jv)),
            pl.BlockSpec(bias_t2.shape, lambda bi, r: (0, 0, 0, 0)),
        ],
        out_specs=pl.BlockSpec((nb, GRID_W, w), lambda bi, r: (bi, r, 0)),
        compiler_params=_params(("arbitrary", "arbitrary"), 48),
        name="nbr_attention",
    )(p_x, p_x, p_x, p_y, p_y, bias_t2)


def _ctx_attn_kernel(q_ref, k_ref, v_ref, o_ref):
    scale = NA_HEAD_DIM ** -0.5
    for p in range(NA_HEADS // 2):
        cols = slice(p * LANES, (p + 1) * LANES)
        k = k_ref[0, :, cols]
        v = v_ref[0, :, cols]
        o = _head_pair_attention(
            q_ref[0, :, cols],
            lambda qm, hh, k=k: [_dot_nt(qm, k) * scale],
            lambda probs, v=v: _dot(probs[0], v))
        o_ref[0, :, cols] = o.astype(BF16)


def _ctx_attention(p_y):
    b, c, _ = p_y.shape
    w = NA_HEADS * NA_HEAD_DIM
    jq, jk, jv = COL_QN // w, COL_KN // w, COL_VN // w
    return pl.pallas_call(
        _ctx_attn_kernel,
        out_shape=jax.ShapeDtypeStruct((b, c, w), BF16),
        grid=(b,),
        in_specs=[
            pl.BlockSpec((1, c, w), lambda bi: (bi, 0, jq)),
            pl.BlockSpec((1, c, w), lambda bi: (bi, 0, jk)),
            pl.BlockSpec((1, c, w), lambda bi: (bi, 0, jv)),
        ],
        out_specs=pl.BlockSpec((1, c, w), lambda bi: (bi, 0, 0)),
        compiler_params=_params(("arbitrary",), 32),
        name="ctx_attention",
    )(p_y, p_y, p_y)


def _fourier_kernel(f_ref, cc_ref, sc_ref, cn_ref, sn_ref, o_ref, ab_scr, *, norm):
    i = pl.program_id(1)
    w = FT_GROUPS * FT_GROUP_DIM

    @pl.when(i == 0)
    def _():
        for g in range(FT_GROUPS):
            u = f_ref[0, :, g * FT_GROUP_DIM:(g + 1) * FT_GROUP_DIM]
            ab_scr[:, g * FT_GROUP_DIM:(g + 1) * FT_GROUP_DIM] = _dot(u, cc_ref[...]).astype(BF16)
            ab_scr[:, w + g * FT_GROUP_DIM:w + (g + 1) * FT_GROUP_DIM] = _dot(u, sc_ref[...]).astype(BF16)

    o = _dot(cn_ref[...], ab_scr[:, :w]) - _dot(sn_ref[...], ab_scr[:, w:])
    o_ref[0] = (o * norm).astype(BF16)


def _dft_tables(n):
    j = jnp.arange(n, dtype=jnp.int32)
    m = (j[:, None] * j[None, :]) % n
    ang = m.astype(F32) * (2.0 * math.pi / n)
    return jnp.cos(ang).astype(BF16), jnp.sin(ang).astype(BF16)


def _fourier_mix(p, tables_n, tables_c, *, tm):
    b, n, _ = p.shape
    w = FT_GROUPS * FT_GROUP_DIM
    cn, sn = tables_n
    cc, sc = tables_c
    jf = COL_F // w
    kern = functools.partial(_fourier_kernel, norm=1.0 / math.sqrt(n * FT_GROUP_DIM))
    return pl.pallas_call(
        kern,
        out_shape=jax.ShapeDtypeStruct((b, n, w), BF16),
        grid=(b, n // tm),
        in_specs=[
            pl.BlockSpec((1, n, w), lambda bi, i: (bi, 0, jf)),
            pl.BlockSpec(cc.shape, lambda bi, i: (0, 0)),
            pl.BlockSpec(sc.shape, lambda bi, i: (0, 0)),
            pl.BlockSpec((tm, n), lambda bi, i: (i, 0)),
            pl.BlockSpec((tm, n), lambda bi, i: (i, 0)),
        ],
        out_specs=pl.BlockSpec((1, tm, w), lambda bi, i: (bi, i, 0)),
        scratch_shapes=[pltpu.VMEM((n, 2 * w), BF16)],
        compiler_params=_params(("arbitrary", "arbitrary"), 48),
        name="fourier_mix",
    )(p, cc, sc, cn, sn)


def _store_token_tiles(ref, value):
    n, d = value.shape
    chunks = d // LANES
    for c in range(chunks):
        ref[pl.ds(c, n, stride=chunks), :] = value[:, c * LANES:(c + 1) * LANES]


def _load_token_tiles(ref, n, chunks):
    return [ref[pl.ds(c, n, stride=chunks), :] for c in range(chunks)]


def _merge_kernel(oa_ref, ob_ref, oc_ref, *rest):
    n_gt = 2 * N_BRANCHES
    gt_refs = rest[:n_gt]
    (x_ref, g1_ref, wb_ref, wo_ref, gn_ref, sc_ref, sh_ref, wrh_ref, wrl_ref, br_ref, c0_ref,
     xo_ref, h_ref, ti_ref, tw_ref, cnt_ref) = rest[n_gt:]
    d = x_ref.shape[1]
    tm = x_ref.shape[0]
    half = d // 2

    @pl.when(pl.program_id(0) == 0)
    def _():
        cnt_ref[...] = c0_ref[...]
    m = [None, None]
    for i, o_ref in enumerate((oa_ref, ob_ref, oc_ref)):
        proj = _dot(o_ref[...], wb_ref[i])
        for hf in range(2):
            term = _sigmoid(gt_refs[2 * i + hf][...].astype(F32)) * proj[:, hf * half:(hf + 1) * half]
            m[hf] = term if m[hf] is None else m[hf] + term
    mix = _dot(jnp.concatenate(m, axis=1).astype(BF16), wo_ref[...])
    x = x_ref[...] + g1_ref[0] * mix
    xo_ref[...] = x
    ms = jnp.mean(x * x, axis=-1, keepdims=True)
    h = x * lax.rsqrt(ms + EPS) * gn_ref[...]
    h = h * (1.0 + sc_ref[0]) + sh_ref[0]
    _store_token_tiles(h_ref, h)
    h_hi = h.astype(BF16)
    h_lo = (h - h_hi.astype(F32)).astype(BF16)
    logits = (_dot(h_hi, wrh_ref[...]) + (_dot(h_hi, wrl_ref[...]) + _dot(h_lo, wrh_ref[...]))
              + br_ref[...])
    lane = lax.broadcasted_iota(jnp.int32, logits.shape, 1).astype(F32)
    cur = logits
    vals, idxs = [], []
    for _ in range(TOP_K):
        mx = jnp.max(cur, axis=-1, keepdims=True)
        ix = jnp.min(jnp.where(cur == mx, lane, float(LANES)), axis=-1, keepdims=True)
        vals.append(mx)
        idxs.append(ix)
        cur = jnp.where(lane == ix, -jnp.inf, cur)
    es = [jnp.exp(v - vals[0]) for v in vals]
    inv = 1.0 / functools.reduce(lambda a, b: a + b, es)
    member = functools.reduce(lambda a, b: a + b, [jnp.where(lane == ix, 1.0, 0.0) for ix in idxs])
    row_i = lax.broadcasted_iota(jnp.int32, (tm, tm), 0)
    col_i = lax.broadcasted_iota(jnp.int32, (tm, tm), 1)
    lower = jnp.where(col_i < row_i, 1.0, 0.0).astype(BF16)
    prefix = _dot(lower, member.astype(BF16)) + cnt_ref[0:1, :]
    ti = jnp.zeros(logits.shape, F32)
    tw = jnp.zeros(logits.shape, F32)
    for k in range(TOP_K):
        rank = jnp.sum(jnp.where(lane == idxs[k], prefix, 0.0), axis=-1, keepdims=True)
        ti = jnp.where(lane == float(k), idxs[k], ti)
        ti = jnp.where(lane == float(TOP_K + k), rank, ti)
        tw = jnp.where(lane == float(k), es[k] * inv, tw)
    ti_ref[...] = ti.astype(jnp.int32)
    tw_ref[...] = tw
    cnt_ref[...] = cnt_ref[...] + jnp.sum(member, axis=0, keepdims=True)


def _merge(oa, ob, oc, p_flat, x_flat, g1, w_branch, w_out, g_ffn, sc2, sh2, wr_hi, wr_lo, b_router,
           counts0, *, rows_per_mod, tm):
    t, d = x_flat.shape
    chunks = d // LANES
    bw = BRANCH_WIDTH
    bpm = rows_per_mod // tm
    row = lambda i: (i, 0)
    mod = lambda i: (i // bpm, 0, 0)
    whole2 = lambda i: (0, 0)
    n_gt = 2 * N_BRANCHES
    gw = N_BRANCHES * d // n_gt
    gate_specs = [pl.BlockSpec((tm, gw), functools.partial(lambda i, j: (i, j), j=COL_GATE // gw + j))
                  for j in range(n_gt)]
    return pl.pallas_call(
        _merge_kernel,
        out_shape=(
            jax.ShapeDtypeStruct((t, d), F32),
            jax.ShapeDtypeStruct((t * chunks, LANES), F32),
            jax.ShapeDtypeStruct((t, LANES), jnp.int32),
            jax.ShapeDtypeStruct((t, LANES), F32),
            jax.ShapeDtypeStruct((8, LANES), F32),
        ),
        grid=(t // tm,),
        in_specs=[
            pl.BlockSpec((tm, bw), row),
            pl.BlockSpec((tm, bw), row),
            pl.BlockSpec((tm, bw), row),
            *gate_specs,
            pl.BlockSpec((tm, d), row),
            pl.BlockSpec((1, 1, d), mod),
            pl.BlockSpec((N_BRANCHES, bw, d), lambda i: (0, 0, 0)),
            pl.BlockSpec((d, d), whole2),
            pl.BlockSpec((1, d), whole2),
            pl.BlockSpec((1, 1, d), mod),
            pl.BlockSpec((1, 1, d), mod),
            pl.BlockSpec((d, LANES), whole2),
            pl.BlockSpec((d, LANES), whole2),
            pl.BlockSpec((1, LANES), whole2),
            pl.BlockSpec((8, LANES), whole2),
        ],
        out_specs=(
            pl.BlockSpec((tm, d), row),
            pl.BlockSpec((tm * chunks, LANES), row),
            pl.BlockSpec((tm, LANES), row),
            pl.BlockSpec((tm, LANES), row),
            pl.BlockSpec((8, LANES), whole2),
        ),
        compiler_params=_params(("arbitrary",), 48),
        name="merge_route",
    )(oa, ob, oc, *([p_flat] * n_gt), x_flat, g1, w_branch, w_out, g_ffn.reshape(1, d), sc2, sh2,
      wr_hi, wr_lo, b_router, counts0)


def _dispatch_kernel(slot_ref, h_ref, xs_in, xs_out, sem):
    del xs_in
    rows = h_ref.shape[0]
    tm = rows // ROW_TILE

    def issue(r, carry):
        src = h_ref.at[pl.ds(pl.multiple_of(r * ROW_TILE, ROW_TILE), ROW_TILE)]
        for k in range(TOP_K):
            s = pl.multiple_of(slot_ref[0, 0, r * TOP_K + k], ROW_TILE)
            pltpu.make_async_copy(src, xs_out.at[pl.ds(s, ROW_TILE)], sem).start(priority=k % 2)
        return carry

    lax.fori_loop(0, tm, issue, 0)
    for k in range(TOP_K):
        pltpu.make_async_copy(h_ref, xs_out.at[pl.ds(0, rows)], sem).wait()


def _dispatch(row_starts, h_tiles, xs_init, *, tm):
    t = h_tiles.shape[0] // ROW_TILE
    return pl.pallas_call(
        _dispatch_kernel,
        out_shape=jax.ShapeDtypeStruct(xs_init.shape, xs_init.dtype),
        grid=(t // tm,),
        in_specs=[
            pl.BlockSpec((1, 1, tm * TOP_K), lambda i: (i, 0, 0), memory_space=pltpu.SMEM),
            pl.BlockSpec((tm * ROW_TILE, LANES), lambda i: (i, 0)),
            pl.BlockSpec(memory_space=pl.ANY),
        ],
        out_specs=pl.BlockSpec(memory_space=pl.ANY),
        scratch_shapes=[pltpu.SemaphoreType.DMA(())],
        input_output_aliases={2: 0},
        compiler_params=_params(("arbitrary",), 32),
        name="moe_dispatch",
    )(row_starts.reshape(t // tm, 1, tm * TOP_K), h_tiles, xs_init)


def _expert_kernel(be_ref, nu_ref, x_ref, wgu_ref, bgu_ref, wd_ref, bd_ref, y_ref, wgu_scr, wd_scr):
    i = pl.program_id(0)
    f = wd_ref.shape[2]
    d = wgu_ref.shape[2]
    new_expert = jnp.logical_or(i == 0, be_ref[i] != be_ref[jnp.maximum(i - 1, 0)])

    @pl.when(new_expert)
    def _():
        chunk = 256
        for c in range(d // chunk):
            wgu_scr[c * chunk:(c + 1) * chunk, :] = wgu_ref[0, 0, c * chunk:(c + 1) * chunk, :].astype(BF16)
        for c in range(f // chunk):
            wd_scr[c * chunk:(c + 1) * chunk, :] = wd_ref[0, 0, c * chunk:(c + 1) * chunk, :].astype(BF16)

    @pl.when(i < nu_ref[0])
    def _():
        x = jnp.concatenate(_load_token_tiles(x_ref, MOE_BLOCK, d // LANES), axis=1).astype(BF16)
        gu = _dot(x, wgu_scr[...]) + bgu_ref[0, 0]
        gate = jnp.minimum(gu[:, :f], SWIGLU_LIMIT)
        up = jnp.clip(gu[:, f:], -SWIGLU_LIMIT, SWIGLU_LIMIT)
        act = gate * _sigmoid(SWIGLU_ALPHA * gate) * (up + 1.0)
        _store_token_tiles(y_ref, _dot(act.astype(BF16), wd_scr[...]) + bd_ref[0, 0])

    @pl.when(i >= nu_ref[0])
    def _():
        y_ref[...] = jnp.zeros(y_ref.shape, y_ref.dtype)


def _expert_ffn(xs, block_e, n_used, layer, wgu, bgu, wd, bd):
    n_blocks = block_e.shape[0]
    depth, e, d, f2 = wgu.shape
    f = f2 // 2
    blk_rows = MOE_BLOCK * d // LANES
    grid_spec = pltpu.PrefetchScalarGridSpec(
        num_scalar_prefetch=2,
        grid=(n_blocks,),
        in_specs=[
            pl.BlockSpec((blk_rows, LANES), lambda i, be, nu: (i, 0)),
            pl.BlockSpec((1, 1, d, f2), lambda i, be, nu: (layer, be[i], 0, 0)),
            pl.BlockSpec((1, 1, 1, f2), lambda i, be, nu: (layer, be[i], 0, 0)),
            pl.BlockSpec((1, 1, f, d), lambda i, be, nu: (layer, be[i], 0, 0)),
            pl.BlockSpec((1, 1, 1, d), lambda i, be, nu: (layer, be[i], 0, 0)),
        ],
        out_specs=pl.BlockSpec((blk_rows, LANES), lambda i, be, nu: (i, 0)),
        scratch_shapes=[pltpu.VMEM((d, f2), BF16), pltpu.VMEM((f, d), BF16)],
    )
    return pl.pallas_call(
        _expert_kernel,
        out_shape=jax.ShapeDtypeStruct((n_blocks * blk_rows, LANES), F32),
        grid_spec=grid_spec,
        compiler_params=_params(("arbitrary",), 56),
        name="expert_ffn",
    )(block_e, n_used, xs, wgu, bgu.reshape(depth, e, 1, f2), wd, bd.reshape(depth, e, 1, d))


def _combine_kernel(slot_ref, y_hbm, tw_ref, x_ref, g2_ref, gf_ref, o_ref, ybuf, sem, *, final):
    tm = x_ref.shape[0]

    def issue(r, carry):
        dst = pl.ds(pl.multiple_of(r * ROW_TILE, ROW_TILE), ROW_TILE)
        for k in range(TOP_K):
            s = pl.multiple_of(slot_ref[0, 0, r * TOP_K + k], ROW_TILE)
            pltpu.make_async_copy(y_hbm.at[pl.ds(s, ROW_TILE)], ybuf.at[k, dst], sem).start(
                priority=k % 2)
        return carry

    lax.fori_loop(0, tm, issue, 0)
    for k in range(TOP_K):
        pltpu.make_async_copy(y_hbm.at[pl.ds(0, tm * ROW_TILE)], ybuf.at[k], sem).wait()
    tw = tw_ref[...]
    lane = lax.broadcasted_iota(jnp.int32, tw.shape, 1)
    wks = [jnp.sum(jnp.where(lane == k, tw, 0.0), axis=-1, keepdims=True) for k in range(TOP_K)]
    chunks = x_ref.shape[1] // LANES
    tiles = [_load_token_tiles(ybuf.at[k], tm, chunks) for k in range(TOP_K)]
    pieces = []
    for c in range(chunks):
        moe = functools.reduce(lambda a, b: a + b, [wks[k] * tiles[k][c] for k in range(TOP_K)])
        cols = slice(c * LANES, (c + 1) * LANES)
        pieces.append(x_ref[:, cols] + g2_ref[0, :, cols] * moe)
    x = jnp.concatenate(pieces, axis=1)
    if final:
        ms = jnp.mean(x * x, axis=-1, keepdims=True)
        x = x * lax.rsqrt(ms + EPS) * gf_ref[...]
    o_ref[...] = x


def _combine(slots, y_slots, top_w, x_flat, g2, final_g, *, rows_per_mod, final, tm):
    t, d = x_flat.shape
    bpm = rows_per_mod // tm
    kern = functools.partial(_combine_kernel, final=final)
    return pl.pallas_call(
        kern,
        out_shape=jax.ShapeDtypeStruct((t, d), F32),
        grid=(t // tm,),
        in_specs=[
            pl.BlockSpec((1, 1, tm * TOP_K), lambda i: (i, 0, 0), memory_space=pltpu.SMEM),
            pl.BlockSpec(memory_space=pl.ANY),
            pl.BlockSpec((tm, LANES), lambda i: (i, 0)),
            pl.BlockSpec((tm, d), lambda i: (i, 0)),
            pl.BlockSpec((1, 1, d), lambda i: (i // bpm, 0, 0)),
            pl.BlockSpec((1, d), lambda i: (0, 0)),
        ],
        out_specs=pl.BlockSpec((tm, d), lambda i: (i, 0)),
        scratch_shapes=[pltpu.VMEM((TOP_K, tm * d // LANES, LANES), F32), pltpu.SemaphoreType.DMA(())],
        compiler_params=_params(("arbitrary",), 32),
        name="moe_combine",
    )(slots.reshape(t // tm, 1, tm * TOP_K), y_slots, top_w, x_flat, g2, final_g.reshape(1, d))


def _route_plan(counts, n_tokens):
    counts = counts.astype(jnp.int32)
    nblk = (counts + MOE_BLOCK - 1) // MOE_BLOCK
    blk_end = jnp.cumsum(nblk)
    base = (blk_end - nblk) * MOE_BLOCK
    n_blocks = -(-(n_tokens * TOP_K) // MOE_BLOCK) + N_EXPERTS
    blocks = jnp.arange(n_blocks, dtype=jnp.int32)
    block_e = jnp.minimum(jnp.sum((blk_end[None, :] <= blocks[:, None]).astype(jnp.int32), axis=1),
                          N_EXPERTS - 1)
    return base, block_e, blk_end[-1:]


def _slots(top_i, base):
    experts = jnp.arange(N_EXPERTS, dtype=jnp.int32)
    e = top_i[:, :TOP_K]
    rank = top_i[:, TOP_K:2 * TOP_K]
    slot = jnp.sum(jnp.where(e[..., None] == experts, base, 0), axis=-1) + rank
    return (slot * ROW_TILE).reshape(-1)


def _moe(streams, counts, xs, layer, final_g, wgu, bgu, wd, bd, *, final):
    n_tokens = sum(st[3].shape[0] for st in streams)
    base, block_e, n_used = _route_plan(counts[0, :N_EXPERTS], n_tokens)
    slots = [_slots(st[1], base) for st in streams]
    for st, slot in zip(streams, slots):
        xs = _dispatch(slot, st[0], xs, tm=min(512, st[0].shape[0] // ROW_TILE))
    y_slots = _expert_ffn(xs, block_e, n_used, layer, wgu, bgu, wd, bd)
    outs = [_combine(slot, y_slots, st[2], st[3], st[4], final_g,
                     rows_per_mod=st[5], final=final, tm=256)
            for st, slot in zip(streams, slots)]
    return outs, xs


def _rope_tables(n):
    t = jnp.arange(n)
    row = (t // GRID_W).astype(F32)
    col = (t % GRID_W).astype(F32)
    quarter = DA_HEAD_DIM // 4
    freqs = ROPE_BASE ** (-jnp.arange(quarter, dtype=F32) / quarter)
    lane = jnp.arange(LANES)
    pos = jnp.where(((lane % DA_HEAD_DIM) // (DA_HEAD_DIM // 2))[None, :] == 0, row[:, None], col[:, None])
    ang = pos * freqs[lane % quarter][None, :]
    sign = jnp.where((lane % (DA_HEAD_DIM // 2)) < quarter, -1.0, 1.0)[None, :]
    return jnp.cos(ang), jnp.sin(ang) * sign


def _nbr_bias_table(rpb):
    col = jnp.arange(GRID_W)
    col_start = jnp.clip(col - NA_KW // 2, 0, GRID_W - NA_KW)
    mask = (col[None, :] >= col_start[:, None]) & (col[None, :] < col_start[:, None] + NA_KW)
    dc = jnp.clip(col[None, :] - col[:, None], 1 - NA_KW, NA_KW - 1) + NA_KW - 1
    tab = jnp.where(mask[None, None], rpb.astype(F32)[:, :, dc], NEG_BIG)
    return jnp.concatenate([tab[:, :-1], tab[:, 1:]], axis=-1)


def _split_hi_lo(w):
    hi = w.astype(BF16)
    return hi, (w - hi.astype(F32)).astype(BF16)


def kernel(x, c, ctx, c_ctx, w_mod, b_mod, norm_mix_g, norm_ffn_g, w_in, da_lambda, da_subln_g,
           na_rpb, w_branch, w_out, w_router, b_router, w_gate_up, b_gate_up, w_down, b_down,
           final_g):
    b, s, d = x.shape
    assert d == ROW_TILE * LANES, "token rows must fill exactly one (8, 128) f32 tile"
    cl = ctx.shape[1]
    depth = w_mod.shape[0]
    tx, ty = b * s, b * cl

    n_rows = -(-(b + 1) // 8) * 8
    c_rows = jnp.concatenate([c, c_ctx[None], jnp.zeros((n_rows - b - 1, d), F32)], axis=0)
    mod = _modulation(c_rows, w_mod, b_mod)

    cos_t, sin_t = _rope_tables(s)
    tab_s = _dft_tables(s)
    tab_c = _dft_tables(cl)
    tab_g = _dft_tables(FT_GROUP_DIM)

    xf = x.reshape(tx, d)
    yf = ctx.reshape(ty, d)
    max_blocks = -(-((tx + ty) * TOP_K) // MOE_BLOCK) + N_EXPERTS
    xs_buf = jnp.zeros((max_blocks * MOE_BLOCK * ROW_TILE, LANES), F32)
    for l in range(depth):
        last = l == depth - 1
        lam_init = 0.8 - 0.6 * math.exp(-0.3 * l)
        mods = mod[l].reshape(n_rows, N_MOD, d)
        mx = [mods[:b, i].reshape(b, 1, d) for i in range(N_MOD)]
        my = [mods[b:b + 1, i].reshape(1, 1, d) for i in range(N_MOD)]
        w_in_l = w_in[l].astype(BF16)
        wb_l = w_branch[l].astype(BF16)
        wo_l = w_out[l].astype(BF16)
        wrh_l, wrl_l = _split_hi_lo(jnp.pad(w_router[l], ((0, 0), (0, LANES - N_EXPERTS))))
        br_l = jnp.pad(b_router[l], (0, LANES - N_EXPERTS), constant_values=NEG_BIG).reshape(1, LANES)
        zero_counts = jnp.zeros((8, LANES), F32)

        px = _in_proj(xf, norm_mix_g[l], mx[1], mx[0], w_in_l, cos_t, sin_t,
                      rows_per_mod=s, seq=s, rope=True, tm=min(2048, s))
        py = _in_proj(yf, norm_mix_g[l], my[1], my[0], w_in_l, cos_t, sin_t,
                      rows_per_mod=ty, seq=ty, rope=False, tm=min(1024, ty))
        px3 = px.reshape(b, s, IN_COLS)
        py3 = py.reshape(b, cl, IN_COLS)

        oa_x = _diff_attention(px3, [px3, py3], da_lambda[l], da_subln_g[l], lam_init, tq=512)
        ob_x = _nbr_attention(px3, py3, _nbr_bias_table(na_rpb[l]))
        oc_x = _fourier_mix(px3, tab_s, tab_g, tm=512)
        streams = []
        counts = zero_counts
        if not last:
            oa_y = _diff_attention(py3, [py3], da_lambda[l], da_subln_g[l], lam_init, tq=cl)
            ob_y = _ctx_attention(py3)
            oc_y = _fourier_mix(py3, tab_c, tab_g, tm=cl)
            yf, hy, tiy, twy, counts = _merge(
                oa_y.reshape(ty, -1), ob_y.reshape(ty, -1), oc_y.reshape(ty, -1), py, yf, my[2],
                wb_l, wo_l, norm_ffn_g[l], my[4], my[3], wrh_l, wrl_l, br_l, counts,
                rows_per_mod=ty, tm=256)
            streams.append((hy, tiy, twy, yf, my[5], ty))
        xf, hx, tix, twx, counts = _merge(
            oa_x.reshape(tx, -1), ob_x.reshape(tx, -1), oc_x.reshape(tx, -1), px, xf, mx[2],
            wb_l, wo_l, norm_ffn_g[l], mx[4], mx[3], wrh_l, wrl_l, br_l, counts,
            rows_per_mod=s, tm=512)
        streams.append((hx, tix, twx, xf, mx[5], s))

        outs, xs_buf = _moe(streams, counts, xs_buf, l, final_g, w_gate_up, b_gate_up, w_down, b_down,
                            final=last)
        xf = outs[-1]
        if not last:
            yf = outs[0]
    return xf.reshape(b, s, d)
```

```python
import functools
import math

import jax
import jax.numpy as jnp
from jax import lax
from jax.experimental import pallas as pl
from jax.experimental.pallas import tpu as pltpu

F32 = jnp.float32
BF16 = jnp.bfloat16

EPS = 1e-6
GRID_W = 64
N_MOD = 6
DA_HEADS = 4
DA_HEAD_DIM = 64
ROPE_BASE = 10000.0
NA_HEADS = 8
NA_HEAD_DIM = 64
NA_KH = 8
NA_KW = 16
FT_GROUPS = 4
FT_GROUP_DIM = 128
BRANCH_WIDTH = 512
N_BRANCHES = 3
N_EXPERTS = 32
TOP_K = 4
SWIGLU_LIMIT = 7.0
SWIGLU_ALPHA = 1.702

LANES = 128
MOE_BLOCK = 512
ROW_TILE = 8
NEG_BIG = -1e30
MIB = 1024 * 1024

COL_QA = 0
COL_KA = 512
COL_VA = 1024
COL_QN = 1536
COL_KN = 2048
COL_VN = 2560
COL_F = 3072
COL_GATE = 3584
IN_COLS = 6656


def _params(sem, vmem_mib):
    return pltpu.CompilerParams(dimension_semantics=sem, vmem_limit_bytes=vmem_mib * MIB)


def _sigmoid(v):
    return 0.5 * jnp.tanh(0.5 * v) + 0.5


def _dot(a, b):
    return jnp.dot(a, b, preferred_element_type=F32)


def _dot_nt(a, b):
    return lax.dot_general(a, b, (((1,), (1,)), ((), ())), preferred_element_type=F32)


def _mod_kernel(c_ref, w_ref, b_ref, o_ref):
    c = c_ref[...]
    s = c * _sigmoid(c)
    o_ref[0] = _dot(s.astype(BF16), w_ref[0].astype(BF16)) + b_ref[0]


def _modulation(c_rows, w_mod, b_mod):
    depth, d, n = w_mod.shape
    rows = c_rows.shape[0]
    tn = 1536
    return pl.pallas_call(
        _mod_kernel,
        out_shape=jax.ShapeDtypeStruct((depth, rows, n), F32),
        grid=(depth, n // tn),
        in_specs=[
            pl.BlockSpec((rows, d), lambda l, j: (0, 0)),
            pl.BlockSpec((1, d, tn), lambda l, j: (l, 0, j)),
            pl.BlockSpec((1, 1, tn), lambda l, j: (l, 0, j)),
        ],
        out_specs=pl.BlockSpec((1, rows, tn), lambda l, j: (l, 0, j)),
        compiler_params=_params(("arbitrary", "arbitrary"), 40),
        name="modulation",
    )(c_rows, w_mod, b_mod.reshape(depth, 1, n))


def _rope128(v, cos, sin_signed, first_half):
    partner = jnp.where(first_half, pltpu.roll(v, LANES - 16, axis=1), pltpu.roll(v, 16, axis=1))
    return v * cos + partner * sin_signed


def _in_proj_kernel(x_ref, g_ref, sc_ref, sh_ref, w_ref, cos_ref, sin_ref, o_ref, h_scr, *, rope, tn):
    j = pl.program_id(1)

    @pl.when(j == 0)
    def _():
        x = x_ref[...]
        ms = jnp.mean(x * x, axis=-1, keepdims=True)
        h = x * lax.rsqrt(ms + EPS) * g_ref[...]
        h = h * (1.0 + sc_ref[0]) + sh_ref[0]
        h_scr[...] = h.astype(BF16)

    acc = _dot(h_scr[...], w_ref[...])
    if rope:
        jq = COL_QA // tn
        jk = COL_KA // tn
        is_rot = jnp.logical_or(j == jq, j == jk)

        @pl.when(is_rot)
        def _():
            cos = cos_ref[...]
            sin = sin_ref[...]
            lane = lax.broadcasted_iota(jnp.int32, (1, LANES), 1)
            first_half = (lane % 32) < 16
            pieces = [
                _rope128(acc[:, k * LANES:(k + 1) * LANES], cos, sin, first_half)
                for k in range(tn // LANES)
            ]
            o_ref[...] = jnp.concatenate(pieces, axis=1).astype(BF16)

        @pl.when(jnp.logical_not(is_rot))
        def _():
            o_ref[...] = acc.astype(BF16)
    else:
        o_ref[...] = acc.astype(BF16)


def _in_proj(x_flat, g, sc, sh, w_bf16, cos_t, sin_t, *, rows_per_mod, seq, rope, tm):
    t, d = x_flat.shape
    n = w_bf16.shape[1]
    tn = 512
    bpm = rows_per_mod // tm
    bps = seq // tm
    kern = functools.partial(_in_proj_kernel, rope=rope, tn=tn)
    return pl.pallas_call(
        kern,
        out_shape=jax.ShapeDtypeStruct((t, n), BF16),
        grid=(t // tm, n // tn),
        in_specs=[
            pl.BlockSpec((tm, d), lambda i, j: (i, 0)),
            pl.BlockSpec((1, d), lambda i, j: (0, 0)),
            pl.BlockSpec((1, 1, d), lambda i, j: (i // bpm, 0, 0)),
            pl.BlockSpec((1, 1, d), lambda i, j: (i // bpm, 0, 0)),
            pl.BlockSpec((d, tn), lambda i, j: (0, j)),
            pl.BlockSpec((tm, LANES), lambda i, j: (i % bps, 0)),
            pl.BlockSpec((tm, LANES), lambda i, j: (i % bps, 0)),
        ],
        out_specs=pl.BlockSpec((tm, tn), lambda i, j: (i, j)),
        scratch_shapes=[pltpu.VMEM((tm, d), BF16)],
        compiler_params=_params(("arbitrary", "arbitrary"), 48),
        name="in_proj_rope" if rope else "in_proj",
    )(x_flat, g.reshape(1, d), sc, sh, w_bf16, cos_t, sin_t)


def _diff_attn_kernel(*refs, lam_init, n_kv):
    q_ref = refs[0]
    k_refs = refs[1:1 + n_kv]
    v_refs = refs[1 + n_kv:1 + 2 * n_kv]
    lam_ref, g_ref, o_ref, vext_scr = refs[1 + 2 * n_kv:]
    sizes = [v_ref.shape[1] for v_ref in v_refs]
    offs = [sum(sizes[:n]) for n in range(n_kv)]

    @pl.when(pl.program_id(2) == 0)
    def _():
        for v_ref, off, n in zip(v_refs, offs, sizes):
            vext_scr[off:off + n, :LANES] = v_ref[0]
            vext_scr[off:off + n, LANES:] = jnp.ones((n, LANES), BF16)

    q = q_ref[0] * (DA_HEAD_DIM ** -0.5)
    lane = lax.broadcasted_iota(jnp.int32, (1, LANES), 1)
    masks = [jnp.where(lane < DA_HEAD_DIM, 1.0, 0.0).astype(BF16),
             jnp.where(lane >= DA_HEAD_DIM, 1.0, 0.0).astype(BF16)]
    ks = [k_ref[0] for k_ref in k_refs]
    scores = [[_dot_nt(q * m, k) for k in ks] for m in masks]
    maxes = [functools.reduce(jnp.maximum, [jnp.max(p, axis=-1, keepdims=True) for p in parts])
             for parts in scores]
    outs = []
    for parts, mx in zip(scores, maxes):
        oe = None
        for p, off, n in zip(parts, offs, sizes):
            term = _dot(jnp.exp(p - mx).astype(BF16), vext_scr[off:off + n, :])
            oe = term if oe is None else oe + term
        outs.append(oe[:, :LANES] * (1.0 / oe[:, LANES:LANES + 1]))
    lp = lam_ref[...]
    t1 = jnp.sum(lp[0:1] * lp[1:2], axis=-1, keepdims=True)
    t2 = jnp.sum(lp[2:3] * lp[3:4], axis=-1, keepdims=True)
    lam = jnp.exp(t1) - jnp.exp(t2) + lam_init
    o = outs[0] - lam * outs[1]
    ms = jnp.mean(o * o, axis=-1, keepdims=True)
    o = o * lax.rsqrt(ms + EPS) * g_ref[...] * (1.0 - lam_init)
    o_ref[0] = o.astype(BF16)


def _diff_attention(p_q, kv_sources, lam_params, subln_g, lam_init, *, tq):
    b, nq, _ = p_q.shape
    qa0, ka0, va0 = COL_QA // LANES, COL_KA // LANES, COL_VA // LANES
    n_kv = len(kv_sources)
    in_specs = [pl.BlockSpec((1, tq, LANES), lambda bi, h, i: (bi, i, qa0 + h))]
    for src in kv_sources:
        in_specs.append(pl.BlockSpec((1, src.shape[1], LANES), lambda bi, h, i: (bi, 0, ka0 + h)))
    for src in kv_sources:
        in_specs.append(pl.BlockSpec((1, src.shape[1], LANES), lambda bi, h, i: (bi, 0, va0 + h)))
    in_specs.append(pl.BlockSpec((4, DA_HEAD_DIM), lambda bi, h, i: (0, 0)))
    in_specs.append(pl.BlockSpec((1, LANES), lambda bi, h, i: (0, 0)))
    kern = functools.partial(_diff_attn_kernel, lam_init=lam_init, n_kv=n_kv)
    return pl.pallas_call(
        kern,
        out_shape=jax.ShapeDtypeStruct((b, nq, DA_HEADS * LANES), BF16),
        grid=(b, DA_HEADS, nq // tq),
        in_specs=in_specs,
        out_specs=pl.BlockSpec((1, tq, LANES), lambda bi, h, i: (bi, i, h)),
        scratch_shapes=[pltpu.VMEM((sum(src.shape[1] for src in kv_sources), 2 * LANES), BF16)],
        compiler_params=_params(("arbitrary", "arbitrary", "arbitrary"), 48),
        name="diff_attention",
    )(p_q, *kv_sources, *kv_sources, lam_params, subln_g.reshape(1, LANES))


def _head_masks():
    lane = lax.broadcasted_iota(jnp.int32, (1, LANES), 1)
    return [jnp.where((lane // NA_HEAD_DIM) == hh, 1.0, 0.0).astype(BF16) for hh in range(2)]


def _softmax_stage(score_parts_per_head):
    maxes = [functools.reduce(jnp.maximum, [jnp.max(p, axis=-1, keepdims=True) for p in parts])
             for parts in score_parts_per_head]
    exps = [[jnp.exp(p - m) for p in parts] for parts, m in zip(score_parts_per_head, maxes)]
    invs = [1.0 / functools.reduce(lambda a, b: a + b, [jnp.sum(e, axis=-1, keepdims=True) for e in es])
            for es in exps]
    return exps, invs


def _head_pair_attention(q_pair, score_fn, value_fn):
    lane = lax.broadcasted_iota(jnp.int32, (1, LANES), 1)
    masks = _head_masks()
    exps, invs = _softmax_stage([score_fn(q_pair * masks[hh], hh) for hh in range(2)])
    outs = [value_fn([e.astype(BF16) for e in es]) * inv for es, inv in zip(exps, invs)]
    return jnp.where(lane < NA_HEAD_DIM, outs[0], outs[1])


def _nbr_attn_kernel(q_ref, kx_ref, vx_ref, kc_ref, vc_ref, bias_ref, o_ref, *, rows):
    r = pl.program_id(1)
    scale = NA_HEAD_DIM ** -0.5
    rs = jnp.clip(r - NA_KH // 2, 0, rows - NA_KH)
    dr0 = rs - r + NA_KH - 1
    win = NA_KH * GRID_W
    start = pl.multiple_of(rs * GRID_W, GRID_W)
    lane = lax.broadcasted_iota(jnp.int32, (1, LANES), 1)
    masks = _head_masks()
    n_pairs = NA_HEADS // 2
    cols = [slice(p * LANES, (p + 1) * LANES) for p in range(n_pairs)]
    nb = q_ref.shape[0]
    scores = []
    for bb in range(nb):
        for p in range(n_pairs):
            q_pair = q_ref[bb, :, cols[p]] * scale
            k_win = kx_ref[bb, pl.ds(start, win), cols[p]]
            k_ctx = kc_ref[bb, :, cols[p]]
            for hh in range(2):
                qm = q_pair * masks[hh]
                bias = jnp.concatenate(
                    [bias_ref[2 * p + hh, dr0 + 2 * jj] for jj in range(NA_KH // 2)], axis=1)
                scores.append([_dot_nt(qm, k_win) + bias, _dot_nt(qm, k_ctx)])
    exps, invs = _softmax_stage(scores)
    for bb in range(nb):
        for p in range(n_pairs):
            v_win = vx_ref[bb, pl.ds(start, win), cols[p]]
            v_ctx = vc_ref[bb, :, cols[p]]
            outs = []
            for hh in range(2):
                idx = (bb * n_pairs + p) * 2 + hh
                e_win, e_ctx = exps[idx]
                o = _dot(e_win.astype(BF16), v_win) + _dot(e_ctx.astype(BF16), v_ctx)
                outs.append(o * invs[idx])
            o_ref[bb, :, cols[p]] = jnp.where(lane < NA_HEAD_DIM, outs[0], outs[1]).astype(BF16)


def _nbr_attention(p_x, p_y, bias_t2):
    b, s, _ = p_x.shape
    c = p_y.shape[1]
    rows = s // GRID_W
    w = NA_HEADS * NA_HEAD_DIM
    jq, jk, jv = COL_QN // w, COL_KN // w, COL_VN // w
    kern = functools.partial(_nbr_attn_kernel, rows=rows)
    nb = 2 if b % 2 == 0 else 1
    return pl.pallas_call(
        kern,
        out_shape=jax.ShapeDtypeStruct((b, s, w), BF16),
        grid=(b // nb, rows),
        in_specs=[
            pl.BlockSpec((nb, GRID_W, w), lambda bi, r: (bi, r, jq)),
            pl.BlockSpec((nb, s, w), lambda bi, r: (bi, 0, jk)),
            pl.BlockSpec((nb, s, w), lambda bi, r: (bi, 0, jv)),
            pl.BlockSpec((nb, c, w), lambda bi, r: (bi, 0, jk)),
            pl.BlockSpec((nb, c, w), lambda bi, r: (bi, 0, jv)),
            pl.BlockSpec(bias_t2.shape, lambda bi, r: (0, 0, 0, 0)),
        ],
        out_specs=pl.BlockSpec((nb, GRID_W, w), lambda bi, r: (bi, r, 0)),
        compiler_params=_params(("arbitrary", "arbitrary"), 48),
        name="nbr_attention",
    )(p_x, p_x, p_x, p_y, p_y, bias_t2)


def _ctx_attn_kernel(q_ref, k_ref, v_ref, o_ref):
    scale = NA_HEAD_DIM ** -0.5
    for p in range(NA_HEADS // 2):
        cols = slice(p * LANES, (p + 1) * LANES)
        k = k_ref[0, :, cols]
        v = v_ref[0, :, cols]
        o = _head_pair_attention(
            q_ref[0, :, cols],
            lambda qm, hh, k=k: [_dot_nt(qm, k) * scale],
            lambda probs, v=v: _dot(probs[0], v))
        o_ref[0, :, cols] = o.astype(BF16)


def _ctx_attention(p_y):
    b, c, _ = p_y.shape
    w = NA_HEADS * NA_HEAD_DIM
    jq, jk, jv = COL_QN // w, COL_KN // w, COL_VN // w
    return pl.pallas_call(
        _ctx_attn_kernel,
        out_shape=jax.ShapeDtypeStruct((b, c, w), BF16),
        grid=(b,),
        in_specs=[
            pl.BlockSpec((1, c, w), lambda bi: (bi, 0, jq)),
            pl.BlockSpec((1, c, w), lambda bi: (bi, 0, jk)),
            pl.BlockSpec((1, c, w), lambda bi: (bi, 0, jv)),
        ],
        out_specs=pl.BlockSpec((1, c, w), lambda bi: (bi, 0, 0)),
        compiler_params=_params(("arbitrary",), 32),
        name="ctx_attention",
    )(p_y, p_y, p_y)


def _fourier_kernel(f_ref, cc_ref, sc_ref, cn_ref, sn_ref, o_ref, ab_scr, *, norm):
    i = pl.program_id(1)
    w = FT_GROUPS * FT_GROUP_DIM

    @pl.when(i == 0)
    def _():
        for g in range(FT_GROUPS):
            u = f_ref[0, :, g * FT_GROUP_DIM:(g + 1) * FT_GROUP_DIM]
            ab_scr[:, g * FT_GROUP_DIM:(g + 1) * FT_GROUP_DIM] = _dot(u, cc_ref[...]).astype(BF16)
            ab_scr[:, w + g * FT_GROUP_DIM:w + (g + 1) * FT_GROUP_DIM] = _dot(u, sc_ref[...]).astype(BF16)

    o = _dot(cn_ref[...], ab_scr[:, :w]) - _dot(sn_ref[...], ab_scr[:, w:])
    o_ref[0] = (o * norm).astype(BF16)


def _dft_tables(n):
    j = jnp.arange(n, dtype=jnp.int32)
    m = (j[:, None] * j[None, :]) % n
    ang = m.astype(F32) * (2.0 * math.pi / n)
    return jnp.cos(ang).astype(BF16), jnp.sin(ang).astype(BF16)


def _fourier_mix(p, tables_n, tables_c, *, tm):
    b, n, _ = p.shape
    w = FT_GROUPS * FT_GROUP_DIM
    cn, sn = tables_n
    cc, sc = tables_c
    jf = COL_F // w
    kern = functools.partial(_fourier_kernel, norm=1.0 / math.sqrt(n * FT_GROUP_DIM))
    return pl.pallas_call(
        kern,
        out_shape=jax.ShapeDtypeStruct((b, n, w), BF16),
        grid=(b, n // tm),
        in_specs=[
            pl.BlockSpec((1, n, w), lambda bi, i: (bi, 0, jf)),
            pl.BlockSpec(cc.shape, lambda bi, i: (0, 0)),
            pl.BlockSpec(sc.shape, lambda bi, i: (0, 0)),
            pl.BlockSpec((tm, n), lambda bi, i: (i, 0)),
            pl.BlockSpec((tm, n), lambda bi, i: (i, 0)),
        ],
        out_specs=pl.BlockSpec((1, tm, w), lambda bi, i: (bi, i, 0)),
        scratch_shapes=[pltpu.VMEM((n, 2 * w), BF16)],
        compiler_params=_params(("arbitrary", "arbitrary"), 48),
        name="fourier_mix",
    )(p, cc, sc, cn, sn)


def _store_token_tiles(ref, value):
    n, d = value.shape
    chunks = d // LANES
    for c in range(chunks):
        ref[pl.ds(c, n, stride=chunks), :] = value[:, c * LANES:(c + 1) * LANES]


def _load_token_tiles(ref, n, chunks):
    return [ref[pl.ds(c, n, stride=chunks), :] for c in range(chunks)]


def _pack_bf16_pair(lo, hi):
    ulo = lax.bitcast_convert_type(lo.astype(BF16).astype(F32), jnp.uint32)
    uhi = lax.bitcast_convert_type(hi.astype(BF16).astype(F32), jnp.uint32)
    return uhi | (ulo >> 16)


def _unpack_bf16_pair(w):
    lo = lax.bitcast_convert_type(w << 16, F32)
    hi = lax.bitcast_convert_type(w & jnp.uint32(0xFFFF0000), F32)
    return lo, hi


def _merge_kernel(oa_ref, ob_ref, oc_ref, *rest):
    n_gt = 2 * N_BRANCHES
    gt_refs = rest[:n_gt]
    (x_ref, g1_ref, wb_ref, wo_ref, gn_ref, sc_ref, sh_ref, wrh_ref, wrl_ref, br_ref, c0_ref,
     xo_ref, h_ref, ti_ref, tw_ref, cnt_ref) = rest[n_gt:]
    d = x_ref.shape[1]
    tm = x_ref.shape[0]
    half = d // 2

    @pl.when(pl.program_id(0) == 0)
    def _():
        cnt_ref[...] = c0_ref[...]
    m = [None, None]
    for i, o_ref in enumerate((oa_ref, ob_ref, oc_ref)):
        proj = _dot(o_ref[...], wb_ref[i])
        for hf in range(2):
            term = _sigmoid(gt_refs[2 * i + hf][...].astype(F32)) * proj[:, hf * half:(hf + 1) * half]
            m[hf] = term if m[hf] is None else m[hf] + term
    mix = _dot(jnp.concatenate(m, axis=1).astype(BF16), wo_ref[...])
    x = x_ref[...] + g1_ref[0] * mix
    xo_ref[...] = x
    ms = jnp.mean(x * x, axis=-1, keepdims=True)
    h = x * lax.rsqrt(ms + EPS) * gn_ref[...]
    h = h * (1.0 + sc_ref[0]) + sh_ref[0]
    _store_token_tiles(h_ref, h)
    h_hi = h.astype(BF16)
    h_lo = (h - h_hi.astype(F32)).astype(BF16)
    logits = (_dot(h_hi, wrh_ref[...]) + (_dot(h_hi, wrl_ref[...]) + _dot(h_lo, wrh_ref[...]))
              + br_ref[...])
    lane = lax.broadcasted_iota(jnp.int32, logits.shape, 1).astype(F32)
    cur = logits
    vals, idxs = [], []
    for _ in range(TOP_K):
        mx = jnp.max(cur, axis=-1, keepdims=True)
        ix = jnp.min(jnp.where(cur == mx, lane, float(LANES)), axis=-1, keepdims=True)
        vals.append(mx)
        idxs.append(ix)
        cur = jnp.where(lane == ix, -jnp.inf, cur)
    es = [jnp.exp(v - vals[0]) for v in vals]
    inv = 1.0 / functools.reduce(lambda a, b: a + b, es)
    member = functools.reduce(lambda a, b: a + b, [jnp.where(lane == ix, 1.0, 0.0) for ix in idxs])
    row_i = lax.broadcasted_iota(jnp.int32, (tm, tm), 0)
    col_i = lax.broadcasted_iota(jnp.int32, (tm, tm), 1)
    lower = jnp.where(col_i < row_i, 1.0, 0.0).astype(BF16)
    prefix = _dot(lower, member.astype(BF16)) + cnt_ref[0:1, :]
    ti = jnp.zeros(logits.shape, F32)
    tw = jnp.zeros(logits.shape, F32)
    for k in range(TOP_K):
        rank = jnp.sum(jnp.where(lane == idxs[k], prefix, 0.0), axis=-1, keepdims=True)
        ti = jnp.where(lane == float(k), idxs[k], ti)
        ti = jnp.where(lane == float(TOP_K + k), rank, ti)
        tw = jnp.where(lane == float(k), es[k] * inv, tw)
    ti_ref[...] = ti.astype(jnp.int32)
    tw_ref[...] = tw
    cnt_ref[...] = cnt_ref[...] + jnp.sum(member, axis=0, keepdims=True)


def _merge(oa, ob, oc, p_flat, x_flat, g1, w_branch, w_out, g_ffn, sc2, sh2, wr_hi, wr_lo, b_router,
           counts0, *, rows_per_mod, tm):
    t, d = x_flat.shape
    chunks = d // LANES
    bw = BRANCH_WIDTH
    bpm = rows_per_mod // tm
    row = lambda i: (i, 0)
    mod = lambda i: (i // bpm, 0, 0)
    whole2 = lambda i: (0, 0)
    n_gt = 2 * N_BRANCHES
    gw = N_BRANCHES * d // n_gt
    gate_specs = [pl.BlockSpec((tm, gw), functools.partial(lambda i, j: (i, j), j=COL_GATE // gw + j))
                  for j in range(n_gt)]
    return pl.pallas_call(
        _merge_kernel,
        out_shape=(
            jax.ShapeDtypeStruct((t, d), F32),
            jax.ShapeDtypeStruct((t * chunks, LANES), F32),
            jax.ShapeDtypeStruct((t, LANES), jnp.int32),
            jax.ShapeDtypeStruct((t, LANES), F32),
            jax.ShapeDtypeStruct((8, LANES), F32),
        ),
        grid=(t // tm,),
        in_specs=[
            pl.BlockSpec((tm, bw), row),
            pl.BlockSpec((tm, bw), row),
            pl.BlockSpec((tm, bw), row),
            *gate_specs,
            pl.BlockSpec((tm, d), row),
            pl.BlockSpec((1, 1, d), mod),
            pl.BlockSpec((N_BRANCHES, bw, d), lambda i: (0, 0, 0)),
            pl.BlockSpec((d, d), whole2),
            pl.BlockSpec((1, d), whole2),
            pl.BlockSpec((1, 1, d), mod),
            pl.BlockSpec((1, 1, d), mod),
            pl.BlockSpec((d, LANES), whole2),
            pl.BlockSpec((d, LANES), whole2),
            pl.BlockSpec((1, LANES), whole2),
            pl.BlockSpec((8, LANES), whole2),
        ],
        out_specs=(
            pl.BlockSpec((tm, d), row),
            pl.BlockSpec((tm * chunks, LANES), row),
            pl.BlockSpec((tm, LANES), row),
            pl.BlockSpec((tm, LANES), row),
            pl.BlockSpec((8, LANES), whole2),
        ),
        compiler_params=_params(("arbitrary",), 48),
        name="merge_route",
    )(oa, ob, oc, *([p_flat] * n_gt), x_flat, g1, w_branch, w_out, g_ffn.reshape(1, d), sc2, sh2,
      wr_hi, wr_lo, b_router, counts0)


def _dispatch_kernel(slot_ref, h_ref, xs_in, xs_out, sem):
    del xs_in
    rows = h_ref.shape[0]
    tm = rows // ROW_TILE

    unroll = 4

    def issue(i, carry):
        for u in range(unroll):
            r = i * unroll + u
            src = h_ref.at[pl.ds(pl.multiple_of(r * ROW_TILE, ROW_TILE), ROW_TILE)]
            for k in range(TOP_K):
                s = pl.multiple_of(slot_ref[0, 0, r * TOP_K + k], ROW_TILE)
                pltpu.make_async_copy(src, xs_out.at[pl.ds(s, ROW_TILE)], sem).start(priority=k % 2)
        return carry

    lax.fori_loop(0, tm // unroll, issue, 0)
    for k in range(TOP_K):
        pltpu.make_async_copy(h_ref, xs_out.at[pl.ds(0, rows)], sem).wait()


def _dispatch(row_starts, h_tiles, xs_init, *, tm):
    t = h_tiles.shape[0] // ROW_TILE
    return pl.pallas_call(
        _dispatch_kernel,
        out_shape=jax.ShapeDtypeStruct(xs_init.shape, xs_init.dtype),
        grid=(t // tm,),
        in_specs=[
            pl.BlockSpec((1, 1, tm * TOP_K), lambda i: (i, 0, 0), memory_space=pltpu.SMEM),
            pl.BlockSpec((tm * ROW_TILE, LANES), lambda i: (i, 0)),
            pl.BlockSpec(memory_space=pl.ANY),
        ],
        out_specs=pl.BlockSpec(memory_space=pl.ANY),
        scratch_shapes=[pltpu.SemaphoreType.DMA(())],
        input_output_aliases={2: 0},
        compiler_params=_params(("arbitrary",), 32),
        name="moe_dispatch",
    )(row_starts.reshape(t // tm, 1, tm * TOP_K), h_tiles, xs_init)


def _expert_kernel(be_ref, nu_ref, x_ref, wgu_ref, bgu_ref, wd_ref, bd_ref, y_ref, wgu_scr, wd_scr):
    i = pl.program_id(0)
    f = wd_ref.shape[2]
    d = wgu_ref.shape[2]
    new_expert = jnp.logical_or(i == 0, be_ref[i] != be_ref[jnp.maximum(i - 1, 0)])

    @pl.when(new_expert)
    def _():
        chunk = 256
        for c in range(d // chunk):
            wgu_scr[c * chunk:(c + 1) * chunk, :] = wgu_ref[0, 0, c * chunk:(c + 1) * chunk, :].astype(BF16)
        for c in range(f // chunk):
            wd_scr[c * chunk:(c + 1) * chunk, :] = wd_ref[0, 0, c * chunk:(c + 1) * chunk, :].astype(BF16)

    @pl.when(i < nu_ref[0])
    def _():
        x = jnp.concatenate(_load_token_tiles(x_ref, MOE_BLOCK, d // LANES), axis=1).astype(BF16)
        gu = _dot(x, wgu_scr[...]) + bgu_ref[0, 0]
        gate = jnp.minimum(gu[:, :f], SWIGLU_LIMIT)
        up = jnp.clip(gu[:, f:], -SWIGLU_LIMIT, SWIGLU_LIMIT)
        act = gate * _sigmoid(SWIGLU_ALPHA * gate) * (up + 1.0)
        y = _dot(act.astype(BF16), wd_scr[...]) + bd_ref[0, 0]
        _store_token_tiles(y_ref, _pack_bf16_pair(y[:, :d // 2], y[:, d // 2:]))

    @pl.when(i >= nu_ref[0])
    def _():
        y_ref[...] = jnp.zeros(y_ref.shape, y_ref.dtype)


def _expert_ffn(xs, block_e, n_used, layer, wgu, bgu, wd, bd):
    n_blocks = block_e.shape[0]
    depth, e, d, f2 = wgu.shape
    f = f2 // 2
    blk_rows = MOE_BLOCK * d // LANES
    grid_spec = pltpu.PrefetchScalarGridSpec(
        num_scalar_prefetch=2,
        grid=(n_blocks,),
        in_specs=[
            pl.BlockSpec((blk_rows, LANES), lambda i, be, nu: (i, 0)),
            pl.BlockSpec((1, 1, d, f2), lambda i, be, nu: (layer, be[i], 0, 0)),
            pl.BlockSpec((1, 1, 1, f2), lambda i, be, nu: (layer, be[i], 0, 0)),
            pl.BlockSpec((1, 1, f, d), lambda i, be, nu: (layer, be[i], 0, 0)),
            pl.BlockSpec((1, 1, 1, d), lambda i, be, nu: (layer, be[i], 0, 0)),
        ],
        out_specs=pl.BlockSpec((blk_rows // 2, LANES), lambda i, be, nu: (i, 0)),
        scratch_shapes=[pltpu.VMEM((d, f2), BF16), pltpu.VMEM((f, d), BF16)],
    )
    return pl.pallas_call(
        _expert_kernel,
        out_shape=jax.ShapeDtypeStruct((n_blocks * blk_rows // 2, LANES), jnp.uint32),
        grid_spec=grid_spec,
        compiler_params=_params(("arbitrary",), 56),
        name="expert_ffn",
    )(block_e, n_used, xs, wgu, bgu.reshape(depth, e, 1, f2), wd, bd.reshape(depth, e, 1, d))


def _combine_kernel(slot_ref, y_hbm, tw_ref, x_ref, g2_ref, gf_ref, o_ref, ybuf, sem, *, final):
    tm = x_ref.shape[0]

    half_tile = ROW_TILE // 2

    unroll = 4

    def issue(i, carry):
        for u in range(unroll):
            r = i * unroll + u
            dst = pl.ds(pl.multiple_of(r * half_tile, half_tile), half_tile)
            for k in range(TOP_K):
                s = lax.shift_right_logical(slot_ref[0, 0, r * TOP_K + k], 1)
                src = pl.ds(pl.multiple_of(s, half_tile), half_tile)
                pltpu.make_async_copy(y_hbm.at[src], ybuf.at[k, dst], sem).start(priority=k % 2)
        return carry

    lax.fori_loop(0, tm // unroll, issue, 0)
    for k in range(TOP_K):
        pltpu.make_async_copy(y_hbm.at[pl.ds(0, tm * half_tile)], ybuf.at[k], sem).wait()
    tw = tw_ref[...]
    lane = lax.broadcasted_iota(jnp.int32, tw.shape, 1)
    wks = [jnp.sum(jnp.where(lane == k, tw, 0.0), axis=-1, keepdims=True) for k in range(TOP_K)]
    tiles = [[_unpack_bf16_pair(w) for w in _load_token_tiles(ybuf.at[k], tm, half_tile)]
             for k in range(TOP_K)]
    pieces = []
    for part in range(2):
        for c in range(half_tile):
            moe = functools.reduce(lambda a, b: a + b,
                                   [wks[k] * tiles[k][c][part] for k in range(TOP_K)])
            col0 = (part * half_tile + c) * LANES
            cols = slice(col0, col0 + LANES)
            pieces.append(x_ref[:, cols] + g2_ref[0, :, cols] * moe)
    x = jnp.concatenate(pieces, axis=1)
    if final:
        ms = jnp.mean(x * x, axis=-1, keepdims=True)
        x = x * lax.rsqrt(ms + EPS) * gf_ref[...]
    o_ref[...] = x


def _combine(slots, y_slots, top_w, x_flat, g2, final_g, *, rows_per_mod, final, tm):
    t, d = x_flat.shape
    bpm = rows_per_mod // tm
    kern = functools.partial(_combine_kernel, final=final)
    return pl.pallas_call(
        kern,
        out_shape=jax.ShapeDtypeStruct((t, d), F32),
        grid=(t // tm,),
        in_specs=[
            pl.BlockSpec((1, 1, tm * TOP_K), lambda i: (i, 0, 0), memory_space=pltpu.SMEM),
            pl.BlockSpec(memory_space=pl.ANY),
            pl.BlockSpec((tm, LANES), lambda i: (i, 0)),
            pl.BlockSpec((tm, d), lambda i: (i, 0)),
            pl.BlockSpec((1, 1, d), lambda i: (i // bpm, 0, 0)),
            pl.BlockSpec((1, d), lambda i: (0, 0)),
        ],
        out_specs=pl.BlockSpec((tm, d), lambda i: (i, 0)),
        scratch_shapes=[pltpu.VMEM((TOP_K, tm * d // LANES // 2, LANES), jnp.uint32),
                        pltpu.SemaphoreType.DMA(())],
        compiler_params=_params(("arbitrary",), 32),
        name="moe_combine",
    )(slots.reshape(t // tm, 1, tm * TOP_K), y_slots, top_w, x_flat, g2, final_g.reshape(1, d))


def _route_plan(counts, n_tokens):
    counts = counts.astype(jnp.int32)
    nblk = (counts + MOE_BLOCK - 1) // MOE_BLOCK
    blk_end = jnp.cumsum(nblk)
    base = (blk_end - nblk) * MOE_BLOCK
    n_blocks = -(-(n_tokens * TOP_K) // MOE_BLOCK) + N_EXPERTS
    blocks = jnp.arange(n_blocks, dtype=jnp.int32)
    block_e = jnp.minimum(jnp.sum((blk_end[None, :] <= blocks[:, None]).astype(jnp.int32), axis=1),
                          N_EXPERTS - 1)
    return base, block_e, blk_end[-1:]


def _slots(top_i, base):
    experts = jnp.arange(N_EXPERTS, dtype=jnp.int32)
    e = top_i[:, :TOP_K]
    rank = top_i[:, TOP_K:2 * TOP_K]
    slot = jnp.sum(jnp.where(e[..., None] == experts, base, 0), axis=-1) + rank
    return (slot * ROW_TILE).reshape(-1)


def _moe(streams, counts, xs, layer, final_g, wgu, bgu, wd, bd, *, final):
    n_tokens = sum(st[3].shape[0] for st in streams)
    base, block_e, n_used = _route_plan(counts[0, :N_EXPERTS], n_tokens)
    slots = [_slots(st[1], base) for st in streams]
    for st, slot in zip(streams, slots):
        xs = _dispatch(slot, st[0], xs, tm=min(512, st[0].shape[0] // ROW_TILE))
    y_slots = _expert_ffn(xs, block_e, n_used, layer, wgu, bgu, wd, bd)
    outs = [_combine(slot, y_slots, st[2], st[3], st[4], final_g,
                     rows_per_mod=st[5], final=final, tm=256)
            for st, slot in zip(streams, slots)]
    return outs, xs


def _rope_tables(n):
    t = jnp.arange(n)
    row = (t // GRID_W).astype(F32)
    col = (t % GRID_W).astype(F32)
    quarter = DA_HEAD_DIM // 4
    freqs = ROPE_BASE ** (-jnp.arange(quarter, dtype=F32) / quarter)
    lane = jnp.arange(LANES)
    pos = jnp.where(((lane % DA_HEAD_DIM) // (DA_HEAD_DIM // 2))[None, :] == 0, row[:, None], col[:, None])
    ang = pos * freqs[lane % quarter][None, :]
    sign = jnp.where((lane % (DA_HEAD_DIM // 2)) < quarter, -1.0, 1.0)[None, :]
    return jnp.cos(ang), jnp.sin(ang) * sign


def _nbr_bias_table(rpb):
    col = jnp.arange(GRID_W)
    col_start = jnp.clip(col - NA_KW // 2, 0, GRID_W - NA_KW)
    mask = (col[None, :] >= col_start[:, None]) & (col[None, :] < col_start[:, None] + NA_KW)
    dc = jnp.clip(col[None, :] - col[:, None], 1 - NA_KW, NA_KW - 1) + NA_KW - 1
    tab = jnp.where(mask[None, None], rpb.astype(F32)[:, :, dc], NEG_BIG)
    return jnp.concatenate([tab[:, :-1], tab[:, 1:]], axis=-1)


def _split_hi_lo(w):
    hi = w.astype(BF16)
    return hi, (w - hi.astype(F32)).astype(BF16)


def kernel(x, c, ctx, c_ctx, w_mod, b_mod, norm_mix_g, norm_ffn_g, w_in, da_lambda, da_subln_g,
           na_rpb, w_branch, w_out, w_router, b_router, w_gate_up, b_gate_up, w_down, b_down,
           final_g):
    b, s, d = x.shape
    assert d == ROW_TILE * LANES, "token rows must fill exactly one (8, 128) f32 tile"
    cl = ctx.shape[1]
    depth = w_mod.shape[0]
    tx, ty = b * s, b * cl

    n_rows = -(-(b + 1) // 8) * 8
    c_rows = jnp.concatenate([c, c_ctx[None], jnp.zeros((n_rows - b - 1, d), F32)], axis=0)
    mod = _modulation(c_rows, w_mod, b_mod)

    cos_t, sin_t = _rope_tables(s)
    tab_s = _dft_tables(s)
    tab_c = _dft_tables(cl)
    tab_g = _dft_tables(FT_GROUP_DIM)

    xf = x.reshape(tx, d)
    yf = ctx.reshape(ty, d)
    max_blocks = -(-((tx + ty) * TOP_K) // MOE_BLOCK) + N_EXPERTS
    xs_buf = jnp.zeros((max_blocks * MOE_BLOCK * ROW_TILE, LANES), F32)
    for l in range(depth):
        last = l == depth - 1
        lam_init = 0.8 - 0.6 * math.exp(-0.3 * l)
        mods = mod[l].reshape(n_rows, N_MOD, d)
        mx = [mods[:b, i].reshape(b, 1, d) for i in range(N_MOD)]
        my = [mods[b:b + 1, i].reshape(1, 1, d) for i in range(N_MOD)]
        w_in_l = w_in[l].astype(BF16)
        wb_l = w_branch[l].astype(BF16)
        wo_l = w_out[l].astype(BF16)
        wrh_l, wrl_l = _split_hi_lo(jnp.pad(w_router[l], ((0, 0), (0, LANES - N_EXPERTS))))
        br_l = jnp.pad(b_router[l], (0, LANES - N_EXPERTS), constant_values=NEG_BIG).reshape(1, LANES)
        zero_counts = jnp.zeros((8, LANES), F32)

        px = _in_proj(xf, norm_mix_g[l], mx[1], mx[0], w_in_l, cos_t, sin_t,
                      rows_per_mod=s, seq=s, rope=True, tm=min(2048, s))
        py = _in_proj(yf, norm_mix_g[l], my[1], my[0], w_in_l, cos_t, sin_t,
                      rows_per_mod=ty, seq=ty, rope=False, tm=min(1024, ty))
        px3 = px.reshape(b, s, IN_COLS)
        py3 = py.reshape(b, cl, IN_COLS)

        oa_x = _diff_attention(px3, [px3, py3], da_lambda[l], da_subln_g[l], lam_init, tq=512)
        ob_x = _nbr_attention(px3, py3, _nbr_bias_table(na_rpb[l]))
        oc_x = _fourier_mix(px3, tab_s, tab_g, tm=512)
        streams = []
        counts = zero_counts
        if not last:
            oa_y = _diff_attention(py3, [py3], da_lambda[l], da_subln_g[l], lam_init, tq=cl)
            ob_y = _ctx_attention(py3)
            oc_y = _fourier_mix(py3, tab_c, tab_g, tm=cl)
            yf, hy, tiy, twy, counts = _merge(
                oa_y.reshape(ty, -1), ob_y.reshape(ty, -1), oc_y.reshape(ty, -1), py, yf, my[2],
                wb_l, wo_l, norm_ffn_g[l], my[4], my[3], wrh_l, wrl_l, br_l, counts,
                rows_per_mod=ty, tm=256)
            streams.append((hy, tiy, twy, yf, my[5], ty))
        xf, hx, tix, twx, counts = _merge(
            oa_x.reshape(tx, -1), ob_x.reshape(tx, -1), oc_x.reshape(tx, -1), px, xf, mx[2],
            wb_l, wo_l, norm_ffn_g[l], mx[4], mx[3], wrh_l, wrl_l, br_l, counts,
            rows_per_mod=s, tm=512)
        streams.append((hx, tix, twx, xf, mx[5], s))

        outs, xs_buf = _moe(streams, counts, xs_buf, l, final_g, w_gate_up, b_gate_up, w_down, b_down,
                            final=last)
        xf = outs[-1]
        if not last:
            yf = outs[0]
    return xf.reshape(b, s, d)
```

```python
import functools
import math

import jax
import jax.numpy as jnp
from jax import lax
from jax.experimental import pallas as pl
from jax.experimental.pallas import tpu as pltpu

F32 = jnp.float32
BF16 = jnp.bfloat16

EPS = 1e-6
GRID_W = 64
N_MOD = 6
DA_HEADS = 4
DA_HEAD_DIM = 64
ROPE_BASE = 10000.0
NA_HEADS = 8
NA_HEAD_DIM = 64
NA_KH = 8
NA_KW = 16
FT_GROUPS = 4
FT_GROUP_DIM = 128
BRANCH_WIDTH = 512
N_BRANCHES = 3
N_EXPERTS = 32
TOP_K = 4
SWIGLU_LIMIT = 7.0
SWIGLU_ALPHA = 1.702

LANES = 128
MOE_BLOCK = 512
ROW_TILE = 8
NEG_BIG = -1e30
MIB = 1024 * 1024

COL_QA = 0
COL_KA = 512
COL_VA = 1024
COL_QN = 1536
COL_KN = 2048
COL_VN = 2560
COL_F = 3072
COL_GATE = 3584
IN_COLS = 6656


def _params(sem, vmem_mib):
    return pltpu.CompilerParams(dimension_semantics=sem, vmem_limit_bytes=vmem_mib * MIB)


def _sigmoid(v):
    return 0.5 * jnp.tanh(0.5 * v) + 0.5


def _dot(a, b):
    return jnp.dot(a, b, preferred_element_type=F32)


def _dot_nt(a, b):
    return lax.dot_general(a, b, (((1,), (1,)), ((), ())), preferred_element_type=F32)


def _mod_kernel(c_ref, w_ref, b_ref, o_ref):
    c = c_ref[...]
    s = c * _sigmoid(c)
    o_ref[0] = _dot(s.astype(BF16), w_ref[0].astype(BF16)) + b_ref[0]


def _modulation(c_rows, w_mod, b_mod):
    depth, d, n = w_mod.shape
    rows = c_rows.shape[0]
    tn = 1536
    return pl.pallas_call(
        _mod_kernel,
        out_shape=jax.ShapeDtypeStruct((depth, rows, n), F32),
        grid=(depth, n // tn),
        in_specs=[
            pl.BlockSpec((rows, d), lambda l, j: (0, 0)),
            pl.BlockSpec((1, d, tn), lambda l, j: (l, 0, j)),
            pl.BlockSpec((1, 1, tn), lambda l, j: (l, 0, j)),
        ],
        out_specs=pl.BlockSpec((1, rows, tn), lambda l, j: (l, 0, j)),
        compiler_params=_params(("arbitrary", "arbitrary"), 40),
        name="modulation",
    )(c_rows, w_mod, b_mod.reshape(depth, 1, n))


def _rope128(v, cos, sin_signed, first_half):
    partner = jnp.where(first_half, pltpu.roll(v, LANES - 16, axis=1), pltpu.roll(v, 16, axis=1))
    return v * cos + partner * sin_signed


def _in_proj_kernel(x_ref, g_ref, sc_ref, sh_ref, w_ref, cos_ref, sin_ref, o_ref, h_scr, *, rope, tn):
    j = pl.program_id(1)

    @pl.when(j == 0)
    def _():
        x = x_ref[...]
        ms = jnp.mean(x * x, axis=-1, keepdims=True)
        h = x * lax.rsqrt(ms + EPS) * g_ref[...]
        h = h * (1.0 + sc_ref[0]) + sh_ref[0]
        h_scr[...] = h.astype(BF16)

    acc = _dot(h_scr[...], w_ref[...])
    if rope:
        jq = COL_QA // tn
        jk = COL_KA // tn
        is_rot = jnp.logical_or(j == jq, j == jk)

        @pl.when(is_rot)
        def _():
            cos = cos_ref[...]
            sin = sin_ref[...]
            lane = lax.broadcasted_iota(jnp.int32, (1, LANES), 1)
            first_half = (lane % 32) < 16
            pieces = [
                _rope128(acc[:, k * LANES:(k + 1) * LANES], cos, sin, first_half)
                for k in range(tn // LANES)
            ]
            o_ref[...] = jnp.concatenate(pieces, axis=1).astype(BF16)

        @pl.when(jnp.logical_not(is_rot))
        def _():
            o_ref[...] = acc.astype(BF16)
    else:
        o_ref[...] = acc.astype(BF16)


def _in_proj(x_flat, g, sc, sh, w_bf16, cos_t, sin_t, *, rows_per_mod, seq, rope, tm):
    t, d = x_flat.shape
    n = w_bf16.shape[1]
    tn = 512
    bpm = rows_per_mod // tm
    bps = seq // tm
    kern = functools.partial(_in_proj_kernel, rope=rope, tn=tn)
    return pl.pallas_call(
        kern,
        out_shape=jax.ShapeDtypeStruct((t, n), BF16),
        grid=(t // tm, n // tn),
        in_specs=[
            pl.BlockSpec((tm, d), lambda i, j: (i, 0)),
            pl.BlockSpec((1, d), lambda i, j: (0, 0)),
            pl.BlockSpec((1, 1, d), lambda i, j: (i // bpm, 0, 0)),
            pl.BlockSpec((1, 1, d), lambda i, j: (i // bpm, 0, 0)),
            pl.BlockSpec((d, tn), lambda i, j: (0, j)),
            pl.BlockSpec((tm, LANES), lambda i, j: (i % bps, 0)),
            pl.BlockSpec((tm, LANES), lambda i, j: (i % bps, 0)),
        ],
        out_specs=pl.BlockSpec((tm, tn), lambda i, j: (i, j)),
        scratch_shapes=[pltpu.VMEM((tm, d), BF16)],
        compiler_params=_params(("arbitrary", "arbitrary"), 48),
        name="in_proj_rope" if rope else "in_proj",
    )(x_flat, g.reshape(1, d), sc, sh, w_bf16, cos_t, sin_t)


def _diff_attn_kernel(*refs, lam_init, n_kv, n_split):
    q_ref = refs[0]
    k_refs = refs[1:1 + n_kv]
    v_refs = refs[1 + n_kv:1 + 2 * n_kv]
    lam_ref, g_ref, o_ref, vext_scr = refs[1 + 2 * n_kv:]
    sizes = [v_ref.shape[1] for v_ref in v_refs]
    offs = [sum(sizes[:n]) for n in range(n_kv)]

    @pl.when(pl.program_id(2) == 0)
    def _():
        for v_ref, off, n in zip(v_refs, offs, sizes):
            vext_scr[off:off + n, :LANES] = v_ref[0]
            vext_scr[off:off + n, LANES:] = jnp.ones((n, LANES), BF16)

    lane = lax.broadcasted_iota(jnp.int32, (1, LANES), 1)
    masks = [jnp.where(lane < DA_HEAD_DIM, 1.0, 0.0).astype(BF16),
             jnp.where(lane >= DA_HEAD_DIM, 1.0, 0.0).astype(BF16)]
    ks = [k_ref[0] for k_ref in k_refs]
    lp = lam_ref[...]
    t1 = jnp.sum(lp[0:1] * lp[1:2], axis=-1, keepdims=True)
    t2 = jnp.sum(lp[2:3] * lp[3:4], axis=-1, keepdims=True)
    lam = jnp.exp(t1) - jnp.exp(t2) + lam_init
    tq = q_ref.shape[1]
    th = tq // n_split
    groups = [q_ref[0, g * th:(g + 1) * th, :] * (DA_HEAD_DIM ** -0.5) for g in range(n_split)]
    scores = [[[_dot_nt(q * m, k) for k in ks] for m in masks] for q in groups]
    maxes = [[functools.reduce(jnp.maximum, [jnp.max(p, axis=-1, keepdims=True) for p in parts])
              for parts in grp] for grp in scores]
    for g in range(n_split):
        outs = []
        for parts, mx in zip(scores[g], maxes[g]):
            oe = None
            for p, off, n in zip(parts, offs, sizes):
                term = _dot(jnp.exp(p - mx).astype(BF16), vext_scr[off:off + n, :])
                oe = term if oe is None else oe + term
            outs.append(oe[:, :LANES] * (1.0 / oe[:, LANES:LANES + 1]))
        o = outs[0] - lam * outs[1]
        ms = jnp.mean(o * o, axis=-1, keepdims=True)
        o = o * lax.rsqrt(ms + EPS) * g_ref[...] * (1.0 - lam_init)
        o_ref[0, g * th:(g + 1) * th, :] = o.astype(BF16)


def _diff_attention(p_q, kv_sources, lam_params, subln_g, lam_init, *, tq, n_split=1):
    b, nq, _ = p_q.shape
    qa0, ka0, va0 = COL_QA // LANES, COL_KA // LANES, COL_VA // LANES
    n_kv = len(kv_sources)
    in_specs = [pl.BlockSpec((1, tq, LANES), lambda bi, h, i: (bi, i, qa0 + h))]
    for src in kv_sources:
        in_specs.append(pl.BlockSpec((1, src.shape[1], LANES), lambda bi, h, i: (bi, 0, ka0 + h)))
    for src in kv_sources:
        in_specs.append(pl.BlockSpec((1, src.shape[1], LANES), lambda bi, h, i: (bi, 0, va0 + h)))
    in_specs.append(pl.BlockSpec((4, DA_HEAD_DIM), lambda bi, h, i: (0, 0)))
    in_specs.append(pl.BlockSpec((1, LANES), lambda bi, h, i: (0, 0)))
    kern = functools.partial(_diff_attn_kernel, lam_init=lam_init, n_kv=n_kv, n_split=n_split)
    return pl.pallas_call(
        kern,
        out_shape=jax.ShapeDtypeStruct((b, nq, DA_HEADS * LANES), BF16),
        grid=(b, DA_HEADS, nq // tq),
        in_specs=in_specs,
        out_specs=pl.BlockSpec((1, tq, LANES), lambda bi, h, i: (bi, i, h)),
        scratch_shapes=[pltpu.VMEM((sum(src.shape[1] for src in kv_sources), 2 * LANES), BF16)],
        compiler_params=_params(("arbitrary", "arbitrary", "arbitrary"), 48),
        name="diff_attention",
    )(p_q, *kv_sources, *kv_sources, lam_params, subln_g.reshape(1, LANES))


def _head_masks():
    lane = lax.broadcasted_iota(jnp.int32, (1, LANES), 1)
    return [jnp.where((lane // NA_HEAD_DIM) == hh, 1.0, 0.0).astype(BF16) for hh in range(2)]


def _softmax_stage(score_parts_per_head):
    maxes = [functools.reduce(jnp.maximum, [jnp.max(p, axis=-1, keepdims=True) for p in parts])
             for parts in score_parts_per_head]
    exps = [[jnp.exp(p - m) for p in parts] for parts, m in zip(score_parts_per_head, maxes)]
    invs = [1.0 / functools.reduce(lambda a, b: a + b, [jnp.sum(e, axis=-1, keepdims=True) for e in es])
            for es in exps]
    return exps, invs


def _head_pair_attention(q_pair, score_fn, value_fn):
    lane = lax.broadcasted_iota(jnp.int32, (1, LANES), 1)
    masks = _head_masks()
    exps, invs = _softmax_stage([score_fn(q_pair * masks[hh], hh) for hh in range(2)])
    outs = [value_fn([e.astype(BF16) for e in es]) * inv for es, inv in zip(exps, invs)]
    return jnp.where(lane < NA_HEAD_DIM, outs[0], outs[1])


def _nbr_attn_kernel(q_ref, kx_ref, vx_ref, kc_ref, vc_ref, bias_ref, o_ref, *, rows):
    r = pl.program_id(1)
    scale = NA_HEAD_DIM ** -0.5
    rs = jnp.clip(r - NA_KH // 2, 0, rows - NA_KH)
    dr0 = rs - r + NA_KH - 1
    win = NA_KH * GRID_W
    start = pl.multiple_of(rs * GRID_W, GRID_W)
    lane = lax.broadcasted_iota(jnp.int32, (1, LANES), 1)
    masks = _head_masks()
    n_pairs = NA_HEADS // 2
    cols = [slice(p * LANES, (p + 1) * LANES) for p in range(n_pairs)]
    nb = q_ref.shape[0]
    scores = []
    for bb in range(nb):
        for p in range(n_pairs):
            q_pair = q_ref[bb, :, cols[p]] * scale
            k_win = kx_ref[bb, pl.ds(start, win), cols[p]]
            k_ctx = kc_ref[bb, :, cols[p]]
            for hh in range(2):
                qm = q_pair * masks[hh]
                bias = jnp.concatenate(
                    [bias_ref[2 * p + hh, dr0 + 2 * jj] for jj in range(NA_KH // 2)], axis=1)
                scores.append([_dot_nt(qm, k_win) + bias, _dot_nt(qm, k_ctx)])
    exps, invs = _softmax_stage(scores)
    for bb in range(nb):
        for p in range(n_pairs):
            v_win = vx_ref[bb, pl.ds(start, win), cols[p]]
            v_ctx = vc_ref[bb, :, cols[p]]
            outs = []
            for hh in range(2):
                idx = (bb * n_pairs + p) * 2 + hh
                e_win, e_ctx = exps[idx]
                o = _dot(e_win.astype(BF16), v_win) + _dot(e_ctx.astype(BF16), v_ctx)
                outs.append(o * invs[idx])
            o_ref[bb, :, cols[p]] = jnp.where(lane < NA_HEAD_DIM, outs[0], outs[1]).astype(BF16)


def _nbr_attention(p_x, p_y, bias_t2):
    b, s, _ = p_x.shape
    c = p_y.shape[1]
    rows = s // GRID_W
    w = NA_HEADS * NA_HEAD_DIM
    jq, jk, jv = COL_QN // w, COL_KN // w, COL_VN // w
    kern = functools.partial(_nbr_attn_kernel, rows=rows)
    nb = 2 if b % 2 == 0 else 1
    return pl.pallas_call(
        kern,
        out_shape=jax.ShapeDtypeStruct((b, s, w), BF16),
        grid=(b // nb, rows),
        in_specs=[
            pl.BlockSpec((nb, GRID_W, w), lambda bi, r: (bi, r, jq)),
            pl.BlockSpec((nb, s, w), lambda bi, r: (bi, 0, jk)),
            pl.BlockSpec((nb, s, w), lambda bi, r: (bi, 0, jv)),
            pl.BlockSpec((nb, c, w), lambda bi, r: (bi, 0, jk)),
            pl.BlockSpec((nb, c, w), lambda bi, r: (bi, 0, jv)),
            pl.BlockSpec(bias_t2.shape, lambda bi, r: (0, 0, 0, 0)),
        ],
        out_specs=pl.BlockSpec((nb, GRID_W, w), lambda bi, r: (bi, r, 0)),
        compiler_params=_params(("arbitrary", "arbitrary"), 48),
        name="nbr_attention",
    )(p_x, p_x, p_x, p_y, p_y, bias_t2)


def _ctx_attn_kernel(q_ref, k_ref, v_ref, o_ref):
    scale = NA_HEAD_DIM ** -0.5
    for p in range(NA_HEADS // 2):
        cols = slice(p * LANES, (p + 1) * LANES)
        k = k_ref[0, :, cols]
        v = v_ref[0, :, cols]
        o = _head_pair_attention(
            q_ref[0, :, cols],
            lambda qm, hh, k=k: [_dot_nt(qm, k) * scale],
            lambda probs, v=v: _dot(probs[0], v))
        o_ref[0, :, cols] = o.astype(BF16)


def _ctx_attention(p_y):
    b, c, _ = p_y.shape
    w = NA_HEADS * NA_HEAD_DIM
    jq, jk, jv = COL_QN // w, COL_KN // w, COL_VN // w
    return pl.pallas_call(
        _ctx_attn_kernel,
        out_shape=jax.ShapeDtypeStruct((b, c, w), BF16),
        grid=(b,),
        in_specs=[
            pl.BlockSpec((1, c, w), lambda bi: (bi, 0, jq)),
            pl.BlockSpec((1, c, w), lambda bi: (bi, 0, jk)),
            pl.BlockSpec((1, c, w), lambda bi: (bi, 0, jv)),
        ],
        out_specs=pl.BlockSpec((1, c, w), lambda bi: (bi, 0, 0)),
        compiler_params=_params(("arbitrary",), 32),
        name="ctx_attention",
    )(p_y, p_y, p_y)


def _fourier_kernel(f_ref, cc_ref, sc_ref, cn_ref, sn_ref, o_ref, ab_scr, *, norm):
    i = pl.program_id(1)
    w = FT_GROUPS * FT_GROUP_DIM

    @pl.when(i == 0)
    def _():
        for g in range(FT_GROUPS):
            u = f_ref[0, :, g * FT_GROUP_DIM:(g + 1) * FT_GROUP_DIM]
            ab_scr[:, g * FT_GROUP_DIM:(g + 1) * FT_GROUP_DIM] = _dot(u, cc_ref[...]).astype(BF16)
            ab_scr[:, w + g * FT_GROUP_DIM:w + (g + 1) * FT_GROUP_DIM] = _dot(u, sc_ref[...]).astype(BF16)

    o = _dot(cn_ref[...], ab_scr[:, :w]) - _dot(sn_ref[...], ab_scr[:, w:])
    o_ref[0] = (o * norm).astype(BF16)


def _dft_tables(n):
    j = jnp.arange(n, dtype=jnp.int32)
    m = (j[:, None] * j[None, :]) % n
    ang = m.astype(F32) * (2.0 * math.pi / n)
    return jnp.cos(ang).astype(BF16), jnp.sin(ang).astype(BF16)


def _fourier_mix(p, tables_n, tables_c, *, tm):
    b, n, _ = p.shape
    w = FT_GROUPS * FT_GROUP_DIM
    cn, sn = tables_n
    cc, sc = tables_c
    jf = COL_F // w
    kern = functools.partial(_fourier_kernel, norm=1.0 / math.sqrt(n * FT_GROUP_DIM))
    return pl.pallas_call(
        kern,
        out_shape=jax.ShapeDtypeStruct((b, n, w), BF16),
        grid=(b, n // tm),
        in_specs=[
            pl.BlockSpec((1, n, w), lambda bi, i: (bi, 0, jf)),
            pl.BlockSpec(cc.shape, lambda bi, i: (0, 0)),
            pl.BlockSpec(sc.shape, lambda bi, i: (0, 0)),
            pl.BlockSpec((tm, n), lambda bi, i: (i, 0)),
            pl.BlockSpec((tm, n), lambda bi, i: (i, 0)),
        ],
        out_specs=pl.BlockSpec((1, tm, w), lambda bi, i: (bi, i, 0)),
        scratch_shapes=[pltpu.VMEM((n, 2 * w), BF16)],
        compiler_params=_params(("arbitrary", "arbitrary"), 48),
        name="fourier_mix",
    )(p, cc, sc, cn, sn)


def _store_token_tiles(ref, value):
    n, d = value.shape
    chunks = d // LANES
    for c in range(chunks):
        ref[pl.ds(c, n, stride=chunks), :] = value[:, c * LANES:(c + 1) * LANES]


def _load_token_tiles(ref, n, chunks):
    return [ref[pl.ds(c, n, stride=chunks), :] for c in range(chunks)]


def _pack_bf16_pair(lo, hi):
    ulo = lax.bitcast_convert_type(lo.astype(BF16).astype(F32), jnp.uint32)
    uhi = lax.bitcast_convert_type(hi.astype(BF16).astype(F32), jnp.uint32)
    return uhi | (ulo >> 16)


def _unpack_bf16_pair(w):
    lo = lax.bitcast_convert_type(w << 16, F32)
    hi = lax.bitcast_convert_type(w & jnp.uint32(0xFFFF0000), F32)
    return lo, hi


def _merge_kernel(oa_ref, ob_ref, oc_ref, *rest):
    n_gt = 2 * N_BRANCHES
    gt_refs = rest[:n_gt]
    (x_ref, g1_ref, wb_ref, wo_ref, gn_ref, sc_ref, sh_ref, wrh_ref, wrl_ref, br_ref, c0_ref,
     xo_ref, h_ref, ti_ref, tw_ref, cnt_ref) = rest[n_gt:]
    d = x_ref.shape[1]
    tm = x_ref.shape[0]
    half = d // 2

    @pl.when(pl.program_id(0) == 0)
    def _():
        cnt_ref[...] = c0_ref[...]
    m = [None, None]
    for i, o_ref in enumerate((oa_ref, ob_ref, oc_ref)):
        proj = _dot(o_ref[...], wb_ref[i])
        for hf in range(2):
            term = _sigmoid(gt_refs[2 * i + hf][...].astype(F32)) * proj[:, hf * half:(hf + 1) * half]
            m[hf] = term if m[hf] is None else m[hf] + term
    mix = _dot(jnp.concatenate(m, axis=1).astype(BF16), wo_ref[...])
    x = x_ref[...] + g1_ref[0] * mix
    xo_ref[...] = x
    ms = jnp.mean(x * x, axis=-1, keepdims=True)
    h = x * lax.rsqrt(ms + EPS) * gn_ref[...]
    h = h * (1.0 + sc_ref[0]) + sh_ref[0]
    _store_token_tiles(h_ref, h)
    h_hi = h.astype(BF16)
    h_lo = (h - h_hi.astype(F32)).astype(BF16)
    logits = (_dot(h_hi, wrh_ref[...]) + (_dot(h_hi, wrl_ref[...]) + _dot(h_lo, wrh_ref[...]))
              + br_ref[...])
    lane = lax.broadcasted_iota(jnp.int32, logits.shape, 1).astype(F32)
    cur = logits
    vals, idxs = [], []
    for _ in range(TOP_K):
        mx = jnp.max(cur, axis=-1, keepdims=True)
        ix = jnp.min(jnp.where(cur == mx, lane, float(LANES)), axis=-1, keepdims=True)
        vals.append(mx)
        idxs.append(ix)
        cur = jnp.where(lane == ix, -jnp.inf, cur)
    es = [jnp.exp(v - vals[0]) for v in vals]
    inv = 1.0 / functools.reduce(lambda a, b: a + b, es)
    member = functools.reduce(lambda a, b: a + b, [jnp.where(lane == ix, 1.0, 0.0) for ix in idxs])
    row_i = lax.broadcasted_iota(jnp.int32, (tm, tm), 0)
    col_i = lax.broadcasted_iota(jnp.int32, (tm, tm), 1)
    lower = jnp.where(col_i < row_i, 1.0, 0.0).astype(BF16)
    prefix = _dot(lower, member.astype(BF16)) + cnt_ref[0:1, :]
    ti = jnp.zeros(logits.shape, F32)
    tw = jnp.zeros(logits.shape, F32)
    for k in range(TOP_K):
        rank = jnp.sum(jnp.where(lane == idxs[k], prefix, 0.0), axis=-1, keepdims=True)
        ti = jnp.where(lane == float(k), idxs[k], ti)
        ti = jnp.where(lane == float(TOP_K + k), rank, ti)
        tw = jnp.where(lane == float(k), es[k] * inv, tw)
    ti_ref[...] = ti.astype(jnp.int32)
    tw_ref[...] = tw
    cnt_ref[...] = cnt_ref[...] + jnp.sum(member, axis=0, keepdims=True)


def _merge(oa, ob, oc, p_flat, x_flat, g1, w_branch, w_out, g_ffn, sc2, sh2, wr_hi, wr_lo, b_router,
           counts0, *, rows_per_mod, tm):
    t, d = x_flat.shape
    chunks = d // LANES
    bw = BRANCH_WIDTH
    bpm = rows_per_mod // tm
    row = lambda i: (i, 0)
    mod = lambda i: (i // bpm, 0, 0)
    whole2 = lambda i: (0, 0)
    n_gt = 2 * N_BRANCHES
    gw = N_BRANCHES * d // n_gt
    gate_specs = [pl.BlockSpec((tm, gw), functools.partial(lambda i, j: (i, j), j=COL_GATE // gw + j))
                  for j in range(n_gt)]
    return pl.pallas_call(
        _merge_kernel,
        out_shape=(
            jax.ShapeDtypeStruct((t, d), F32),
            jax.ShapeDtypeStruct((t * chunks, LANES), F32),
            jax.ShapeDtypeStruct((t, LANES), jnp.int32),
            jax.ShapeDtypeStruct((t, LANES), F32),
            jax.ShapeDtypeStruct((8, LANES), F32),
        ),
        grid=(t // tm,),
        in_specs=[
            pl.BlockSpec((tm, bw), row),
            pl.BlockSpec((tm, bw), row),
            pl.BlockSpec((tm, bw), row),
            *gate_specs,
            pl.BlockSpec((tm, d), row),
            pl.BlockSpec((1, 1, d), mod),
            pl.BlockSpec((N_BRANCHES, bw, d), lambda i: (0, 0, 0)),
            pl.BlockSpec((d, d), whole2),
            pl.BlockSpec((1, d), whole2),
            pl.BlockSpec((1, 1, d), mod),
            pl.BlockSpec((1, 1, d), mod),
            pl.BlockSpec((d, LANES), whole2),
            pl.BlockSpec((d, LANES), whole2),
            pl.BlockSpec((1, LANES), whole2),
            pl.BlockSpec((8, LANES), whole2),
        ],
        out_specs=(
            pl.BlockSpec((tm, d), row),
            pl.BlockSpec((tm * chunks, LANES), row),
            pl.BlockSpec((tm, LANES), row),
            pl.BlockSpec((tm, LANES), row),
            pl.BlockSpec((8, LANES), whole2),
        ),
        compiler_params=_params(("arbitrary",), 48),
        name="merge_route",
    )(oa, ob, oc, *([p_flat] * n_gt), x_flat, g1, w_branch, w_out, g_ffn.reshape(1, d), sc2, sh2,
      wr_hi, wr_lo, b_router, counts0)


def _dispatch_kernel(slot_ref, h_ref, xs_in, xs_out, sem):
    del xs_in
    rows = h_ref.shape[0]
    tm = rows // ROW_TILE

    unroll = 4

    def issue(i, carry):
        for u in range(unroll):
            r = i * unroll + u
            src = h_ref.at[pl.ds(pl.multiple_of(r * ROW_TILE, ROW_TILE), ROW_TILE)]
            for k in range(TOP_K):
                s = pl.multiple_of(slot_ref[0, 0, r * TOP_K + k], ROW_TILE)
                pltpu.make_async_copy(src, xs_out.at[pl.ds(s, ROW_TILE)], sem).start(priority=k % 2)
        return carry

    lax.fori_loop(0, tm // unroll, issue, 0)
    for k in range(TOP_K):
        pltpu.make_async_copy(h_ref, xs_out.at[pl.ds(0, rows)], sem).wait()


def _dispatch(row_starts, h_tiles, xs_init, *, tm):
    t = h_tiles.shape[0] // ROW_TILE
    return pl.pallas_call(
        _dispatch_kernel,
        out_shape=jax.ShapeDtypeStruct(xs_init.shape, xs_init.dtype),
        grid=(t // tm,),
        in_specs=[
            pl.BlockSpec((1, 1, tm * TOP_K), lambda i: (i, 0, 0), memory_space=pltpu.SMEM),
            pl.BlockSpec((tm * ROW_TILE, LANES), lambda i: (i, 0)),
            pl.BlockSpec(memory_space=pl.ANY),
        ],
        out_specs=pl.BlockSpec(memory_space=pl.ANY),
        scratch_shapes=[pltpu.SemaphoreType.DMA(())],
        input_output_aliases={2: 0},
        compiler_params=_params(("arbitrary",), 32),
        name="moe_dispatch",
    )(row_starts.reshape(t // tm, 1, tm * TOP_K), h_tiles, xs_init)


def _expert_kernel(be_ref, nu_ref, nxt_ref, vis_ref, x_ref, wgu_hbm, bgu_ref, wd_hbm, bd_ref, y_ref,
                   wgu_f32, wd_f32, wgu_scr, wd_scr, sems, *, layer):
    i = pl.program_id(0)
    d, f2 = wgu_scr.shape
    f = f2 // 2
    new_expert = jnp.logical_or(i == 0, be_ref[i] != be_ref[jnp.maximum(i - 1, 0)])

    def weight_copies(e, slot):
        return (pltpu.make_async_copy(wgu_hbm.at[layer, e], wgu_f32.at[slot], sems.at[0, slot]),
                pltpu.make_async_copy(wd_hbm.at[layer, e], wd_f32.at[slot], sems.at[1, slot]))

    @pl.when(new_expert)
    def _():
        slot = vis_ref[i] % 2

        @pl.when(i == 0)
        def _():
            for cp in weight_copies(be_ref[0], 0):
                cp.start()

        for cp in weight_copies(be_ref[i], slot):
            cp.wait()

        @pl.when(nxt_ref[i] >= 0)
        def _():
            for cp in weight_copies(nxt_ref[i], 1 - slot):
                cp.start()

        chunk = 256
        for c in range(d // chunk):
            rows = slice(c * chunk, (c + 1) * chunk)
            wgu_scr[rows, :] = wgu_f32[slot, rows, :].astype(BF16)
        for c in range(f // chunk):
            rows = slice(c * chunk, (c + 1) * chunk)
            wd_scr[rows, :] = wd_f32[slot, rows, :].astype(BF16)

    @pl.when(i < nu_ref[0])
    def _():
        x = jnp.concatenate(_load_token_tiles(x_ref, MOE_BLOCK, d // LANES), axis=1).astype(BF16)
        gu = _dot(x, wgu_scr[...]) + bgu_ref[0, 0]
        gate = jnp.minimum(gu[:, :f], SWIGLU_LIMIT)
        up = jnp.clip(gu[:, f:], -SWIGLU_LIMIT, SWIGLU_LIMIT)
        act = gate * _sigmoid(SWIGLU_ALPHA * gate) * (up + 1.0)
        y = _dot(act.astype(BF16), wd_scr[...]) + bd_ref[0, 0]
        _store_token_tiles(y_ref, _pack_bf16_pair(y[:, :d // 2], y[:, d // 2:]))

    @pl.when(i >= nu_ref[0])
    def _():
        y_ref[...] = jnp.zeros(y_ref.shape, y_ref.dtype)


def _expert_ffn(xs, block_e, n_used, layer, wgu, bgu, wd, bd):
    n_blocks = block_e.shape[0]
    depth, e, d, f2 = wgu.shape
    f = f2 // 2
    blk_rows = MOE_BLOCK * d // LANES
    later = jnp.where(block_e[None, :] > block_e[:, None], block_e[None, :], N_EXPERTS)
    nxt = jnp.min(later, axis=1)
    nxt = jnp.where(nxt < N_EXPERTS, nxt, -1).astype(jnp.int32)
    first = jnp.concatenate([jnp.ones((1,), jnp.int32),
                             (block_e[1:] != block_e[:-1]).astype(jnp.int32)])
    vis = (jnp.cumsum(first) - 1).astype(jnp.int32)
    spec = lambda shape, fn: pl.BlockSpec(shape, lambda i, be, nu, nx, vs: fn(i, be))
    grid_spec = pltpu.PrefetchScalarGridSpec(
        num_scalar_prefetch=4,
        grid=(n_blocks,),
        in_specs=[
            spec((blk_rows, LANES), lambda i, be: (i, 0)),
            pl.BlockSpec(memory_space=pl.ANY),
            spec((1, 1, 1, f2), lambda i, be: (layer, be[i], 0, 0)),
            pl.BlockSpec(memory_space=pl.ANY),
            spec((1, 1, 1, d), lambda i, be: (layer, be[i], 0, 0)),
        ],
        out_specs=spec((blk_rows // 2, LANES), lambda i, be: (i, 0)),
        scratch_shapes=[pltpu.VMEM((2, d, f2), F32), pltpu.VMEM((2, f, d), F32),
                        pltpu.VMEM((d, f2), BF16), pltpu.VMEM((f, d), BF16),
                        pltpu.SemaphoreType.DMA((2, 2))],
    )
    return pl.pallas_call(
        functools.partial(_expert_kernel, layer=layer),
        out_shape=jax.ShapeDtypeStruct((n_blocks * blk_rows // 2, LANES), jnp.uint32),
        grid_spec=grid_spec,
        compiler_params=_params(("arbitrary",), 56),
        name="expert_ffn",
    )(block_e, n_used, nxt, vis, xs, wgu, bgu.reshape(depth, e, 1, f2), wd, bd.reshape(depth, e, 1, d))


def _combine_kernel(slot_ref, y_hbm, tw_ref, x_ref, g2_ref, gf_ref, o_ref, ybuf, sem, *, final):
    tm = x_ref.shape[0]

    half_tile = ROW_TILE // 2

    unroll = 4

    def issue(i, carry):
        for u in range(unroll):
            r = i * unroll + u
            dst = pl.ds(pl.multiple_of(r * half_tile, half_tile), half_tile)
            for k in range(TOP_K):
                s = lax.shift_right_logical(slot_ref[0, 0, r * TOP_K + k], 1)
                src = pl.ds(pl.multiple_of(s, half_tile), half_tile)
                pltpu.make_async_copy(y_hbm.at[src], ybuf.at[k, dst], sem).start(priority=k % 2)
        return carry

    lax.fori_loop(0, tm // unroll, issue, 0)
    for k in range(TOP_K):
        pltpu.make_async_copy(y_hbm.at[pl.ds(0, tm * half_tile)], ybuf.at[k], sem).wait()
    tw = tw_ref[...]
    lane = lax.broadcasted_iota(jnp.int32, tw.shape, 1)
    wks = [jnp.sum(jnp.where(lane == k, tw, 0.0), axis=-1, keepdims=True) for k in range(TOP_K)]
    tiles = [[_unpack_bf16_pair(w) for w in _load_token_tiles(ybuf.at[k], tm, half_tile)]
             for k in range(TOP_K)]
    pieces = []
    for part in range(2):
        for c in range(half_tile):
            moe = functools.reduce(lambda a, b: a + b,
                                   [wks[k] * tiles[k][c][part] for k in range(TOP_K)])
            col0 = (part * half_tile + c) * LANES
            cols = slice(col0, col0 + LANES)
            pieces.append(x_ref[:, cols] + g2_ref[0, :, cols] * moe)
    x = jnp.concatenate(pieces, axis=1)
    if final:
        ms = jnp.mean(x * x, axis=-1, keepdims=True)
        x = x * lax.rsqrt(ms + EPS) * gf_ref[...]
    o_ref[...] = x


def _combine(slots, y_slots, top_w, x_flat, g2, final_g, *, rows_per_mod, final, tm):
    t, d = x_flat.shape
    bpm = rows_per_mod // tm
    kern = functools.partial(_combine_kernel, final=final)
    return pl.pallas_call(
        kern,
        out_shape=jax.ShapeDtypeStruct((t, d), F32),
        grid=(t // tm,),
        in_specs=[
            pl.BlockSpec((1, 1, tm * TOP_K), lambda i: (i, 0, 0), memory_space=pltpu.SMEM),
            pl.BlockSpec(memory_space=pl.ANY),
            pl.BlockSpec((tm, LANES), lambda i: (i, 0)),
            pl.BlockSpec((tm, d), lambda i: (i, 0)),
            pl.BlockSpec((1, 1, d), lambda i: (i // bpm, 0, 0)),
            pl.BlockSpec((1, d), lambda i: (0, 0)),
        ],
        out_specs=pl.BlockSpec((tm, d), lambda i: (i, 0)),
        scratch_shapes=[pltpu.VMEM((TOP_K, tm * d // LANES // 2, LANES), jnp.uint32),
                        pltpu.SemaphoreType.DMA(())],
        compiler_params=_params(("arbitrary",), 32),
        name="moe_combine",
    )(slots.reshape(t // tm, 1, tm * TOP_K), y_slots, top_w, x_flat, g2, final_g.reshape(1, d))


def _route_plan(counts, n_tokens):
    counts = counts.astype(jnp.int32)
    nblk = (counts + MOE_BLOCK - 1) // MOE_BLOCK
    blk_end = jnp.cumsum(nblk)
    base = (blk_end - nblk) * MOE_BLOCK
    n_blocks = -(-(n_tokens * TOP_K) // MOE_BLOCK) + N_EXPERTS
    blocks = jnp.arange(n_blocks, dtype=jnp.int32)
    block_e = jnp.minimum(jnp.sum((blk_end[None, :] <= blocks[:, None]).astype(jnp.int32), axis=1),
                          N_EXPERTS - 1)
    return base, block_e, blk_end[-1:]


def _slots(top_i, base):
    experts = jnp.arange(N_EXPERTS, dtype=jnp.int32)
    e = top_i[:, :TOP_K]
    rank = top_i[:, TOP_K:2 * TOP_K]
    slot = jnp.sum(jnp.where(e[..., None] == experts, base, 0), axis=-1) + rank
    return (slot * ROW_TILE).reshape(-1)


def _moe(streams, counts, xs, layer, final_g, wgu, bgu, wd, bd, *, final):
    n_tokens = sum(st[3].shape[0] for st in streams)
    base, block_e, n_used = _route_plan(counts[0, :N_EXPERTS], n_tokens)
    slots = [_slots(st[1], base) for st in streams]
    for st, slot in zip(streams, slots):
        xs = _dispatch(slot, st[0], xs, tm=min(512, st[0].shape[0] // ROW_TILE))
    y_slots = _expert_ffn(xs, block_e, n_used, layer, wgu, bgu, wd, bd)
    outs = [_combine(slot, y_slots, st[2], st[3], st[4], final_g,
                     rows_per_mod=st[5], final=final, tm=256)
            for st, slot in zip(streams, slots)]
    return outs, xs


def _rope_tables(n):
    t = jnp.arange(n)
    row = (t // GRID_W).astype(F32)
    col = (t % GRID_W).astype(F32)
    quarter = DA_HEAD_DIM // 4
    freqs = ROPE_BASE ** (-jnp.arange(quarter, dtype=F32) / quarter)
    lane = jnp.arange(LANES)
    pos = jnp.where(((lane % DA_HEAD_DIM) // (DA_HEAD_DIM // 2))[None, :] == 0, row[:, None], col[:, None])
    ang = pos * freqs[lane % quarter][None, :]
    sign = jnp.where((lane % (DA_HEAD_DIM // 2)) < quarter, -1.0, 1.0)[None, :]
    return jnp.cos(ang), jnp.sin(ang) * sign


def _nbr_bias_table(rpb):
    col = jnp.arange(GRID_W)
    col_start = jnp.clip(col - NA_KW // 2, 0, GRID_W - NA_KW)
    mask = (col[None, :] >= col_start[:, None]) & (col[None, :] < col_start[:, None] + NA_KW)
    dc = jnp.clip(col[None, :] - col[:, None], 1 - NA_KW, NA_KW - 1) + NA_KW - 1
    tab = jnp.where(mask[None, None], rpb.astype(F32)[:, :, dc], NEG_BIG)
    return jnp.concatenate([tab[:, :-1], tab[:, 1:]], axis=-1)


def _split_hi_lo(w):
    hi = w.astype(BF16)
    return hi, (w - hi.astype(F32)).astype(BF16)


def kernel(x, c, ctx, c_ctx, w_mod, b_mod, norm_mix_g, norm_ffn_g, w_in, da_lambda, da_subln_g,
           na_rpb, w_branch, w_out, w_router, b_router, w_gate_up, b_gate_up, w_down, b_down,
           final_g):
    b, s, d = x.shape
    assert d == ROW_TILE * LANES, "token rows must fill exactly one (8, 128) f32 tile"
    cl = ctx.shape[1]
    depth = w_mod.shape[0]
    tx, ty = b * s, b * cl

    n_rows = -(-(b + 1) // 8) * 8
    c_rows = jnp.concatenate([c, c_ctx[None], jnp.zeros((n_rows - b - 1, d), F32)], axis=0)
    mod = _modulation(c_rows, w_mod, b_mod)

    cos_t, sin_t = _rope_tables(s)
    tab_s = _dft_tables(s)
    tab_c = _dft_tables(cl)
    tab_g = _dft_tables(FT_GROUP_DIM)

    xf = x.reshape(tx, d)
    yf = ctx.reshape(ty, d)
    max_blocks = -(-((tx + ty) * TOP_K) // MOE_BLOCK) + N_EXPERTS
    xs_buf = jnp.zeros((max_blocks * MOE_BLOCK * ROW_TILE, LANES), F32)
    for l in range(depth):
        last = l == depth - 1
        lam_init = 0.8 - 0.6 * math.exp(-0.3 * l)
        mods = mod[l].reshape(n_rows, N_MOD, d)
        mx = [mods[:b, i].reshape(b, 1, d) for i in range(N_MOD)]
        my = [mods[b:b + 1, i].reshape(1, 1, d) for i in range(N_MOD)]
        w_in_l = w_in[l].astype(BF16)
        wb_l = w_branch[l].astype(BF16)
        wo_l = w_out[l].astype(BF16)
        wrh_l, wrl_l = _split_hi_lo(jnp.pad(w_router[l], ((0, 0), (0, LANES - N_EXPERTS))))
        br_l = jnp.pad(b_router[l], (0, LANES - N_EXPERTS), constant_values=NEG_BIG).reshape(1, LANES)
        zero_counts = jnp.zeros((8, LANES), F32)

        px = _in_proj(xf, norm_mix_g[l], mx[1], mx[0], w_in_l, cos_t, sin_t,
                      rows_per_mod=s, seq=s, rope=True, tm=min(2048, s))
        py = _in_proj(yf, norm_mix_g[l], my[1], my[0], w_in_l, cos_t, sin_t,
                      rows_per_mod=ty, seq=ty, rope=False, tm=min(1024, ty))
        px3 = px.reshape(b, s, IN_COLS)
        py3 = py.reshape(b, cl, IN_COLS)

        oa_x = _diff_attention(px3, [px3, py3], da_lambda[l], da_subln_g[l], lam_init,
                               tq=min(1024, s), n_split=4)
        ob_x = _nbr_attention(px3, py3, _nbr_bias_table(na_rpb[l]))
        oc_x = _fourier_mix(px3, tab_s, tab_g, tm=512)
        streams = []
        counts = zero_counts
        if not last:
            oa_y = _diff_attention(py3, [py3], da_lambda[l], da_subln_g[l], lam_init, tq=cl)
            ob_y = _ctx_attention(py3)
            oc_y = _fourier_mix(py3, tab_c, tab_g, tm=cl)
            yf, hy, tiy, twy, counts = _merge(
                oa_y.reshape(ty, -1), ob_y.reshape(ty, -1), oc_y.reshape(ty, -1), py, yf, my[2],
                wb_l, wo_l, norm_ffn_g[l], my[4], my[3], wrh_l, wrl_l, br_l, counts,
                rows_per_mod=ty, tm=256)
            streams.append((hy, tiy, twy, yf, my[5], ty))
        xf, hx, tix, twx, counts = _merge(
            oa_x.reshape(tx, -1), ob_x.reshape(tx, -1), oc_x.reshape(tx, -1), px, xf, mx[2],
            wb_l, wo_l, norm_ffn_g[l], mx[4], mx[3], wrh_l, wrl_l, br_l, counts,
            rows_per_mod=s, tm=512)
        streams.append((hx, tix, twx, xf, mx[5], s))

        outs, xs_buf = _moe(streams, counts, xs_buf, l, final_g, w_gate_up, b_gate_up, w_down, b_down,
                            final=last)
        xf = outs[-1]
        if not last:
            yf = outs[0]
    return xf.reshape(b, s, d)
```

```python
import functools
import math

import jax
import jax.numpy as jnp
from jax import lax
from jax.experimental import pallas as pl
from jax.experimental.pallas import tpu as pltpu

F32 = jnp.float32
BF16 = jnp.bfloat16

EPS = 1e-6
GRID_W = 64
N_MOD = 6
DA_HEADS = 4
DA_HEAD_DIM = 64
ROPE_BASE = 10000.0
NA_HEADS = 8
NA_HEAD_DIM = 64
NA_KH = 8
NA_KW = 16
FT_GROUPS = 4
FT_GROUP_DIM = 128
BRANCH_WIDTH = 512
N_BRANCHES = 3
N_EXPERTS = 32
TOP_K = 4
SWIGLU_LIMIT = 7.0
SWIGLU_ALPHA = 1.702

LANES = 128
MOE_BLOCK = 512
ROW_TILE = 8
NEG_BIG = -1e30
MIB = 1024 * 1024

COL_QA = 0
COL_KA = 512
COL_VA = 1024
COL_QN = 1536
COL_KN = 2048
COL_VN = 2560
COL_F = 3072
COL_GATE = 3584
IN_COLS = 6656


def _params(sem, vmem_mib):
    return pltpu.CompilerParams(dimension_semantics=sem, vmem_limit_bytes=vmem_mib * MIB)


def _sigmoid(v):
    return 0.5 * jnp.tanh(0.5 * v) + 0.5


def _dot(a, b):
    return jnp.dot(a, b, preferred_element_type=F32)


def _dot_nt(a, b):
    return lax.dot_general(a, b, (((1,), (1,)), ((), ())), preferred_element_type=F32)


def _mod_kernel(c_ref, w_ref, b_ref, o_ref):
    c = c_ref[...]
    s = c * _sigmoid(c)
    o_ref[0] = _dot(s.astype(BF16), w_ref[0].astype(BF16)) + b_ref[0]


def _modulation(c_rows, w_mod, b_mod):
    depth, d, n = w_mod.shape
    rows = c_rows.shape[0]
    tn = 1536
    return pl.pallas_call(
        _mod_kernel,
        out_shape=jax.ShapeDtypeStruct((depth, rows, n), F32),
        grid=(depth, n // tn),
        in_specs=[
            pl.BlockSpec((rows, d), lambda l, j: (0, 0)),
            pl.BlockSpec((1, d, tn), lambda l, j: (l, 0, j)),
            pl.BlockSpec((1, 1, tn), lambda l, j: (l, 0, j)),
        ],
        out_specs=pl.BlockSpec((1, rows, tn), lambda l, j: (l, 0, j)),
        compiler_params=_params(("arbitrary", "arbitrary"), 40),
        name="modulation",
    )(c_rows, w_mod, b_mod.reshape(depth, 1, n))


def _rope128(v, cos, sin_signed, first_half):
    partner = jnp.where(first_half, pltpu.roll(v, LANES - 16, axis=1), pltpu.roll(v, 16, axis=1))
    return v * cos + partner * sin_signed


def _in_proj_kernel(x_ref, g_ref, sc_ref, sh_ref, w_ref, cos_ref, sin_ref, o_ref, h_scr, *, rope, tn):
    j = pl.program_id(1)

    @pl.when(j == 0)
    def _():
        x = x_ref[...]
        ms = jnp.mean(x * x, axis=-1, keepdims=True)
        h = x * lax.rsqrt(ms + EPS) * g_ref[...]
        h = h * (1.0 + sc_ref[0]) + sh_ref[0]
        h_scr[...] = h.astype(BF16)

    acc = _dot(h_scr[...], w_ref[...])
    if rope:
        jq = COL_QA // tn
        jk = COL_KA // tn
        is_rot = jnp.logical_or(j == jq, j == jk)

        @pl.when(is_rot)
        def _():
            cos = cos_ref[...]
            sin = sin_ref[...]
            lane = lax.broadcasted_iota(jnp.int32, (1, LANES), 1)
            first_half = (lane % 32) < 16
            pieces = [
                _rope128(acc[:, k * LANES:(k + 1) * LANES], cos, sin, first_half)
                for k in range(tn // LANES)
            ]
            o_ref[...] = jnp.concatenate(pieces, axis=1).astype(BF16)

        @pl.when(jnp.logical_not(is_rot))
        def _():
            o_ref[...] = acc.astype(BF16)
    else:
        o_ref[...] = acc.astype(BF16)


def _in_proj(x_flat, g, sc, sh, w_bf16, cos_t, sin_t, *, rows_per_mod, seq, rope, tm):
    t, d = x_flat.shape
    n = w_bf16.shape[1]
    tn = 512
    bpm = rows_per_mod // tm
    bps = seq // tm
    kern = functools.partial(_in_proj_kernel, rope=rope, tn=tn)
    return pl.pallas_call(
        kern,
        out_shape=jax.ShapeDtypeStruct((t, n), BF16),
        grid=(t // tm, n // tn),
        in_specs=[
            pl.BlockSpec((tm, d), lambda i, j: (i, 0)),
            pl.BlockSpec((1, d), lambda i, j: (0, 0)),
            pl.BlockSpec((1, 1, d), lambda i, j: (i // bpm, 0, 0)),
            pl.BlockSpec((1, 1, d), lambda i, j: (i // bpm, 0, 0)),
            pl.BlockSpec((d, tn), lambda i, j: (0, j)),
            pl.BlockSpec((tm, LANES), lambda i, j: (i % bps, 0)),
            pl.BlockSpec((tm, LANES), lambda i, j: (i % bps, 0)),
        ],
        out_specs=pl.BlockSpec((tm, tn), lambda i, j: (i, j)),
        scratch_shapes=[pltpu.VMEM((tm, d), BF16)],
        compiler_params=_params(("arbitrary", "arbitrary"), 48),
        name="in_proj_rope" if rope else "in_proj",
    )(x_flat, g.reshape(1, d), sc, sh, w_bf16, cos_t, sin_t)


def _diff_attn_kernel(*refs, lam_init, n_kv, n_split):
    q_ref = refs[0]
    k_refs = refs[1:1 + n_kv]
    v_refs = refs[1 + n_kv:1 + 2 * n_kv]
    lam_ref, g_ref, o_ref, vext_scr = refs[1 + 2 * n_kv:]
    sizes = [v_ref.shape[1] for v_ref in v_refs]
    offs = [sum(sizes[:n]) for n in range(n_kv)]

    @pl.when(pl.program_id(2) == 0)
    def _():
        for v_ref, off, n in zip(v_refs, offs, sizes):
            vext_scr[off:off + n, :LANES] = v_ref[0]
            vext_scr[off:off + n, LANES:] = jnp.ones((n, LANES), BF16)

    lane = lax.broadcasted_iota(jnp.int32, (1, LANES), 1)
    masks = [jnp.where(lane < DA_HEAD_DIM, 1.0, 0.0).astype(BF16),
             jnp.where(lane >= DA_HEAD_DIM, 1.0, 0.0).astype(BF16)]
    ks = [k_ref[0] for k_ref in k_refs]
    lp = lam_ref[...]
    t1 = jnp.sum(lp[0:1] * lp[1:2], axis=-1, keepdims=True)
    t2 = jnp.sum(lp[2:3] * lp[3:4], axis=-1, keepdims=True)
    lam = jnp.exp(t1) - jnp.exp(t2) + lam_init
    tq = q_ref.shape[1]
    th = tq // n_split
    groups = [q_ref[0, g * th:(g + 1) * th, :] * (DA_HEAD_DIM ** -0.5) for g in range(n_split)]
    scores = [[[_dot_nt(q * m, k) for k in ks] for m in masks] for q in groups]
    maxes = [[functools.reduce(jnp.maximum, [jnp.max(p, axis=-1, keepdims=True) for p in parts])
              for parts in grp] for grp in scores]
    for g in range(n_split):
        outs = []
        for parts, mx in zip(scores[g], maxes[g]):
            oe = None
            for p, off, n in zip(parts, offs, sizes):
                term = _dot(jnp.exp(p - mx).astype(BF16), vext_scr[off:off + n, :])
                oe = term if oe is None else oe + term
            outs.append(oe[:, :LANES] * (1.0 / oe[:, LANES:LANES + 1]))
        o = outs[0] - lam * outs[1]
        ms = jnp.mean(o * o, axis=-1, keepdims=True)
        o = o * lax.rsqrt(ms + EPS) * g_ref[...] * (1.0 - lam_init)
        o_ref[0, g * th:(g + 1) * th, :] = o.astype(BF16)


def _diff_attention(p_q, kv_sources, lam_params, subln_g, lam_init, *, tq, n_split=1):
    b, nq, _ = p_q.shape
    qa0, ka0, va0 = COL_QA // LANES, COL_KA // LANES, COL_VA // LANES
    n_kv = len(kv_sources)
    in_specs = [pl.BlockSpec((1, tq, LANES), lambda bi, h, i: (bi, i, qa0 + h))]
    for src in kv_sources:
        in_specs.append(pl.BlockSpec((1, src.shape[1], LANES), lambda bi, h, i: (bi, 0, ka0 + h)))
    for src in kv_sources:
        in_specs.append(pl.BlockSpec((1, src.shape[1], LANES), lambda bi, h, i: (bi, 0, va0 + h)))
    in_specs.append(pl.BlockSpec((4, DA_HEAD_DIM), lambda bi, h, i: (0, 0)))
    in_specs.append(pl.BlockSpec((1, LANES), lambda bi, h, i: (0, 0)))
    kern = functools.partial(_diff_attn_kernel, lam_init=lam_init, n_kv=n_kv, n_split=n_split)
    return pl.pallas_call(
        kern,
        out_shape=jax.ShapeDtypeStruct((b, nq, DA_HEADS * LANES), BF16),
        grid=(b, DA_HEADS, nq // tq),
        in_specs=in_specs,
        out_specs=pl.BlockSpec((1, tq, LANES), lambda bi, h, i: (bi, i, h)),
        scratch_shapes=[pltpu.VMEM((sum(src.shape[1] for src in kv_sources), 2 * LANES), BF16)],
        compiler_params=_params(("arbitrary", "arbitrary", "arbitrary"), 48),
        name="diff_attention",
    )(p_q, *kv_sources, *kv_sources, lam_params, subln_g.reshape(1, LANES))


def _head_masks():
    lane = lax.broadcasted_iota(jnp.int32, (1, LANES), 1)
    return [jnp.where((lane // NA_HEAD_DIM) == hh, 1.0, 0.0).astype(BF16) for hh in range(2)]


def _softmax_stage(score_parts_per_head):
    maxes = [functools.reduce(jnp.maximum, [jnp.max(p, axis=-1, keepdims=True) for p in parts])
             for parts in score_parts_per_head]
    exps = [[jnp.exp(p - m) for p in parts] for parts, m in zip(score_parts_per_head, maxes)]
    invs = [1.0 / functools.reduce(lambda a, b: a + b, [jnp.sum(e, axis=-1, keepdims=True) for e in es])
            for es in exps]
    return exps, invs


def _head_pair_attention(q_pair, score_fn, value_fn):
    lane = lax.broadcasted_iota(jnp.int32, (1, LANES), 1)
    masks = _head_masks()
    exps, invs = _softmax_stage([score_fn(q_pair * masks[hh], hh) for hh in range(2)])
    outs = [value_fn([e.astype(BF16) for e in es]) * inv for es, inv in zip(exps, invs)]
    return jnp.where(lane < NA_HEAD_DIM, outs[0], outs[1])


def _nbr_attn_kernel(q_ref, kx_ref, vx_ref, kc_ref, vc_ref, bias_ref, o_ref, *, rows):
    r = pl.program_id(1)
    scale = NA_HEAD_DIM ** -0.5
    rs = jnp.clip(r - NA_KH // 2, 0, rows - NA_KH)
    dr0 = rs - r + NA_KH - 1
    win = NA_KH * GRID_W
    start = pl.multiple_of(rs * GRID_W, GRID_W)
    lane = lax.broadcasted_iota(jnp.int32, (1, LANES), 1)
    masks = _head_masks()
    n_pairs = NA_HEADS // 2
    cols = [slice(p * LANES, (p + 1) * LANES) for p in range(n_pairs)]
    nb = q_ref.shape[0]
    scores = []
    for bb in range(nb):
        for p in range(n_pairs):
            q_pair = q_ref[bb, :, cols[p]] * scale
            k_win = kx_ref[bb, pl.ds(start, win), cols[p]]
            k_ctx = kc_ref[bb, :, cols[p]]
            for hh in range(2):
                qm = q_pair * masks[hh]
                bias = jnp.concatenate(
                    [bias_ref[2 * p + hh, dr0 + 2 * jj] for jj in range(NA_KH // 2)], axis=1)
                scores.append([_dot_nt(qm, k_win) + bias, _dot_nt(qm, k_ctx)])
    exps, invs = _softmax_stage(scores)
    for bb in range(nb):
        for p in range(n_pairs):
            v_win = vx_ref[bb, pl.ds(start, win), cols[p]]
            v_ctx = vc_ref[bb, :, cols[p]]
            outs = []
            for hh in range(2):
                idx = (bb * n_pairs + p) * 2 + hh
                e_win, e_ctx = exps[idx]
                o = _dot(e_win.astype(BF16), v_win) + _dot(e_ctx.astype(BF16), v_ctx)
                outs.append(o * invs[idx])
            o_ref[bb, :, cols[p]] = jnp.where(lane < NA_HEAD_DIM, outs[0], outs[1]).astype(BF16)


def _nbr_attention(p_x, p_y, bias_t2):
    b, s, _ = p_x.shape
    c = p_y.shape[1]
    rows = s // GRID_W
    w = NA_HEADS * NA_HEAD_DIM
    jq, jk, jv = COL_QN // w, COL_KN // w, COL_VN // w
    kern = functools.partial(_nbr_attn_kernel, rows=rows)
    nb = 2 if b % 2 == 0 else 1
    return pl.pallas_call(
        kern,
        out_shape=jax.ShapeDtypeStruct((b, s, w), BF16),
        grid=(b // nb, rows),
        in_specs=[
            pl.BlockSpec((nb, GRID_W, w), lambda bi, r: (bi, r, jq)),
            pl.BlockSpec((nb, s, w), lambda bi, r: (bi, 0, jk)),
            pl.BlockSpec((nb, s, w), lambda bi, r: (bi, 0, jv)),
            pl.BlockSpec((nb, c, w), lambda bi, r: (bi, 0, jk)),
            pl.BlockSpec((nb, c, w), lambda bi, r: (bi, 0, jv)),
            pl.BlockSpec(bias_t2.shape, lambda bi, r: (0, 0, 0, 0)),
        ],
        out_specs=pl.BlockSpec((nb, GRID_W, w), lambda bi, r: (bi, r, 0)),
        compiler_params=_params(("arbitrary", "arbitrary"), 48),
        name="nbr_attention",
    )(p_x, p_x, p_x, p_y, p_y, bias_t2)


def _ctx_attn_kernel(q_ref, k_ref, v_ref, o_ref):
    scale = NA_HEAD_DIM ** -0.5
    for p in range(NA_HEADS // 2):
        cols = slice(p * LANES, (p + 1) * LANES)
        k = k_ref[0, :, cols]
        v = v_ref[0, :, cols]
        o = _head_pair_attention(
            q_ref[0, :, cols],
            lambda qm, hh, k=k: [_dot_nt(qm, k) * scale],
            lambda probs, v=v: _dot(probs[0], v))
        o_ref[0, :, cols] = o.astype(BF16)


def _ctx_attention(p_y):
    b, c, _ = p_y.shape
    w = NA_HEADS * NA_HEAD_DIM
    jq, jk, jv = COL_QN // w, COL_KN // w, COL_VN // w
    return pl.pallas_call(
        _ctx_attn_kernel,
        out_shape=jax.ShapeDtypeStruct((b, c, w), BF16),
        grid=(b,),
        in_specs=[
            pl.BlockSpec((1, c, w), lambda bi: (bi, 0, jq)),
            pl.BlockSpec((1, c, w), lambda bi: (bi, 0, jk)),
            pl.BlockSpec((1, c, w), lambda bi: (bi, 0, jv)),
        ],
        out_specs=pl.BlockSpec((1, c, w), lambda bi: (bi, 0, 0)),
        compiler_params=_params(("arbitrary",), 32),
        name="ctx_attention",
    )(p_y, p_y, p_y)


def _fourier_kernel(f_ref, cc_ref, sc_ref, cn_ref, sn_ref, o_ref, ab_scr, *, norm):
    i = pl.program_id(1)
    w = FT_GROUPS * FT_GROUP_DIM

    @pl.when(i == 0)
    def _():
        for g in range(FT_GROUPS):
            u = f_ref[0, :, g * FT_GROUP_DIM:(g + 1) * FT_GROUP_DIM]
            ab_scr[:, g * FT_GROUP_DIM:(g + 1) * FT_GROUP_DIM] = _dot(u, cc_ref[...]).astype(BF16)
            ab_scr[:, w + g * FT_GROUP_DIM:w + (g + 1) * FT_GROUP_DIM] = _dot(u, sc_ref[...]).astype(BF16)

    o = _dot(cn_ref[...], ab_scr[:, :w]) - _dot(sn_ref[...], ab_scr[:, w:])
    o_ref[0] = (o * norm).astype(BF16)


def _dft_tables(n):
    j = jnp.arange(n, dtype=jnp.int32)
    m = (j[:, None] * j[None, :]) % n
    ang = m.astype(F32) * (2.0 * math.pi / n)
    return jnp.cos(ang).astype(BF16), jnp.sin(ang).astype(BF16)


def _fourier_mix(p, tables_n, tables_c, *, tm):
    b, n, _ = p.shape
    w = FT_GROUPS * FT_GROUP_DIM
    cn, sn = tables_n
    cc, sc = tables_c
    jf = COL_F // w
    kern = functools.partial(_fourier_kernel, norm=1.0 / math.sqrt(n * FT_GROUP_DIM))
    return pl.pallas_call(
        kern,
        out_shape=jax.ShapeDtypeStruct((b, n, w), BF16),
        grid=(b, n // tm),
        in_specs=[
            pl.BlockSpec((1, n, w), lambda bi, i: (bi, 0, jf)),
            pl.BlockSpec(cc.shape, lambda bi, i: (0, 0)),
            pl.BlockSpec(sc.shape, lambda bi, i: (0, 0)),
            pl.BlockSpec((tm, n), lambda bi, i: (i, 0)),
            pl.BlockSpec((tm, n), lambda bi, i: (i, 0)),
        ],
        out_specs=pl.BlockSpec((1, tm, w), lambda bi, i: (bi, i, 0)),
        scratch_shapes=[pltpu.VMEM((n, 2 * w), BF16)],
        compiler_params=_params(("arbitrary", "arbitrary"), 48),
        name="fourier_mix",
    )(p, cc, sc, cn, sn)


def _store_token_tiles(ref, value):
    n, d = value.shape
    chunks = d // LANES
    for c in range(chunks):
        ref[pl.ds(c, n, stride=chunks), :] = value[:, c * LANES:(c + 1) * LANES]


def _load_token_tiles(ref, n, chunks):
    return [ref[pl.ds(c, n, stride=chunks), :] for c in range(chunks)]


def _pack_bf16_pair(lo, hi):
    ulo = lax.bitcast_convert_type(lo.astype(BF16).astype(F32), jnp.uint32)
    uhi = lax.bitcast_convert_type(hi.astype(BF16).astype(F32), jnp.uint32)
    return uhi | (ulo >> 16)


def _unpack_bf16_pair(w):
    lo = lax.bitcast_convert_type(w << 16, F32)
    hi = lax.bitcast_convert_type(w & jnp.uint32(0xFFFF0000), F32)
    return lo, hi


def _merge_kernel(oa_ref, ob_ref, oc_ref, *rest):
    n_gt = 2 * N_BRANCHES
    gt_refs = rest[:n_gt]
    (x_ref, g1_ref, wb_ref, wo_ref, gn_ref, sc_ref, sh_ref, wrh_ref, wrl_ref, br_ref, c0_ref,
     xo_ref, h_ref, ti_ref, tw_ref, cnt_ref) = rest[n_gt:]
    d = x_ref.shape[1]
    tm = x_ref.shape[0]
    half = d // 2

    @pl.when(pl.program_id(0) == 0)
    def _():
        cnt_ref[...] = c0_ref[...]
    m = [None, None]
    for i, o_ref in enumerate((oa_ref, ob_ref, oc_ref)):
        proj = _dot(o_ref[...], wb_ref[i])
        for hf in range(2):
            term = _sigmoid(gt_refs[2 * i + hf][...].astype(F32)) * proj[:, hf * half:(hf + 1) * half]
            m[hf] = term if m[hf] is None else m[hf] + term
    mix = _dot(jnp.concatenate(m, axis=1).astype(BF16), wo_ref[...])
    x = x_ref[...] + g1_ref[0] * mix
    xo_ref[...] = x
    ms = jnp.mean(x * x, axis=-1, keepdims=True)
    h = x * lax.rsqrt(ms + EPS) * gn_ref[...]
    h = h * (1.0 + sc_ref[0]) + sh_ref[0]
    _store_token_tiles(h_ref, _pack_bf16_pair(h[:, :half], h[:, half:]))
    h_hi = h.astype(BF16)
    h_lo = (h - h_hi.astype(F32)).astype(BF16)
    logits = (_dot(h_hi, wrh_ref[...]) + (_dot(h_hi, wrl_ref[...]) + _dot(h_lo, wrh_ref[...]))
              + br_ref[...])
    lane = lax.broadcasted_iota(jnp.int32, logits.shape, 1).astype(F32)
    cur = logits
    vals, idxs = [], []
    for _ in range(TOP_K):
        mx = jnp.max(cur, axis=-1, keepdims=True)
        ix = jnp.min(jnp.where(cur == mx, lane, float(LANES)), axis=-1, keepdims=True)
        vals.append(mx)
        idxs.append(ix)
        cur = jnp.where(lane == ix, -jnp.inf, cur)
    es = [jnp.exp(v - vals[0]) for v in vals]
    inv = 1.0 / functools.reduce(lambda a, b: a + b, es)
    member = functools.reduce(lambda a, b: a + b, [jnp.where(lane == ix, 1.0, 0.0) for ix in idxs])
    row_i = lax.broadcasted_iota(jnp.int32, (tm, tm), 0)
    col_i = lax.broadcasted_iota(jnp.int32, (tm, tm), 1)
    lower = jnp.where(col_i < row_i, 1.0, 0.0).astype(BF16)
    prefix = _dot(lower, member.astype(BF16)) + cnt_ref[0:1, :]
    ti = jnp.zeros(logits.shape, F32)
    tw = jnp.zeros(logits.shape, F32)
    for k in range(TOP_K):
        rank = jnp.sum(jnp.where(lane == idxs[k], prefix, 0.0), axis=-1, keepdims=True)
        ti = jnp.where(lane == float(k), idxs[k], ti)
        ti = jnp.where(lane == float(TOP_K + k), rank, ti)
        tw = jnp.where(lane == float(k), es[k] * inv, tw)
    ti_ref[...] = ti.astype(jnp.int32)
    tw_ref[...] = tw
    cnt_ref[...] = cnt_ref[...] + jnp.sum(member, axis=0, keepdims=True)


def _merge(oa, ob, oc, p_flat, x_flat, g1, w_branch, w_out, g_ffn, sc2, sh2, wr_hi, wr_lo, b_router,
           counts0, *, rows_per_mod, tm):
    t, d = x_flat.shape
    chunks = d // 2 // LANES
    bw = BRANCH_WIDTH
    bpm = rows_per_mod // tm
    row = lambda i: (i, 0)
    mod = lambda i: (i // bpm, 0, 0)
    whole2 = lambda i: (0, 0)
    n_gt = 2 * N_BRANCHES
    gw = N_BRANCHES * d // n_gt
    gate_specs = [pl.BlockSpec((tm, gw), functools.partial(lambda i, j: (i, j), j=COL_GATE // gw + j))
                  for j in range(n_gt)]
    return pl.pallas_call(
        _merge_kernel,
        out_shape=(
            jax.ShapeDtypeStruct((t, d), F32),
            jax.ShapeDtypeStruct((t * chunks, LANES), jnp.uint32),
            jax.ShapeDtypeStruct((t, LANES), jnp.int32),
            jax.ShapeDtypeStruct((t, LANES), F32),
            jax.ShapeDtypeStruct((8, LANES), F32),
        ),
        grid=(t // tm,),
        in_specs=[
            pl.BlockSpec((tm, bw), row),
            pl.BlockSpec((tm, bw), row),
            pl.BlockSpec((tm, bw), row),
            *gate_specs,
            pl.BlockSpec((tm, d), row),
            pl.BlockSpec((1, 1, d), mod),
            pl.BlockSpec((N_BRANCHES, bw, d), lambda i: (0, 0, 0)),
            pl.BlockSpec((d, d), whole2),
            pl.BlockSpec((1, d), whole2),
            pl.BlockSpec((1, 1, d), mod),
            pl.BlockSpec((1, 1, d), mod),
            pl.BlockSpec((d, LANES), whole2),
            pl.BlockSpec((d, LANES), whole2),
            pl.BlockSpec((1, LANES), whole2),
            pl.BlockSpec((8, LANES), whole2),
        ],
        out_specs=(
            pl.BlockSpec((tm, d), row),
            pl.BlockSpec((tm * chunks, LANES), row),
            pl.BlockSpec((tm, LANES), row),
            pl.BlockSpec((tm, LANES), row),
            pl.BlockSpec((8, LANES), whole2),
        ),
        compiler_params=_params(("arbitrary",), 48),
        name="merge_route",
    )(oa, ob, oc, *([p_flat] * n_gt), x_flat, g1, w_branch, w_out, g_ffn.reshape(1, d), sc2, sh2,
      wr_hi, wr_lo, b_router, counts0)


def _dispatch_kernel(slot_ref, h_ref, xs_in, xs_out, sem):
    del xs_in
    rows = h_ref.shape[0]
    half_tile = ROW_TILE // 2
    tm = rows // half_tile
    unroll = 4

    def issue(i, carry):
        for u in range(unroll):
            r = i * unroll + u
            src = h_ref.at[pl.ds(pl.multiple_of(r * half_tile, half_tile), half_tile)]
            for k in range(TOP_K):
                s = lax.shift_right_logical(slot_ref[0, 0, r * TOP_K + k], 1)
                dst = xs_out.at[pl.ds(pl.multiple_of(s, half_tile), half_tile)]
                pltpu.make_async_copy(src, dst, sem).start(priority=k % 2)
        return carry

    lax.fori_loop(0, tm // unroll, issue, 0)
    for k in range(TOP_K):
        pltpu.make_async_copy(h_ref, xs_out.at[pl.ds(0, rows)], sem).wait()


def _dispatch(row_starts, h_tiles, xs_init, *, tm):
    half_tile = ROW_TILE // 2
    t = h_tiles.shape[0] // half_tile
    return pl.pallas_call(
        _dispatch_kernel,
        out_shape=jax.ShapeDtypeStruct(xs_init.shape, xs_init.dtype),
        grid=(t // tm,),
        in_specs=[
            pl.BlockSpec((1, 1, tm * TOP_K), lambda i: (i, 0, 0), memory_space=pltpu.SMEM),
            pl.BlockSpec((tm * half_tile, LANES), lambda i: (i, 0)),
            pl.BlockSpec(memory_space=pl.ANY),
        ],
        out_specs=pl.BlockSpec(memory_space=pl.ANY),
        scratch_shapes=[pltpu.SemaphoreType.DMA(())],
        input_output_aliases={2: 0},
        compiler_params=_params(("arbitrary",), 32),
        name="moe_dispatch",
    )(row_starts.reshape(t // tm, 1, tm * TOP_K), h_tiles, xs_init)


def _expert_kernel(be_ref, nu_ref, nxt_ref, vis_ref, x_ref, wgu_hbm, bgu_ref, wd_hbm, bd_ref, y_ref,
                   wgu_f32, wd_f32, wgu_scr, wd_scr, sems, *, layer):
    i = pl.program_id(0)
    d, f2 = wgu_scr.shape
    f = f2 // 2
    new_expert = jnp.logical_or(i == 0, be_ref[i] != be_ref[jnp.maximum(i - 1, 0)])

    def weight_copies(e, slot):
        return (pltpu.make_async_copy(wgu_hbm.at[layer, e], wgu_f32.at[slot], sems.at[0, slot]),
                pltpu.make_async_copy(wd_hbm.at[layer, e], wd_f32.at[slot], sems.at[1, slot]))

    @pl.when(new_expert)
    def _():
        slot = vis_ref[i] % 2

        @pl.when(i == 0)
        def _():
            for cp in weight_copies(be_ref[0], 0):
                cp.start()

        for cp in weight_copies(be_ref[i], slot):
            cp.wait()

        @pl.when(nxt_ref[i] >= 0)
        def _():
            for cp in weight_copies(nxt_ref[i], 1 - slot):
                cp.start()

        chunk = 256
        for c in range(d // chunk):
            rows = slice(c * chunk, (c + 1) * chunk)
            wgu_scr[rows, :] = wgu_f32[slot, rows, :].astype(BF16)
        for c in range(f // chunk):
            rows = slice(c * chunk, (c + 1) * chunk)
            wd_scr[rows, :] = wd_f32[slot, rows, :].astype(BF16)

    @pl.when(i < nu_ref[0])
    def _():
        pairs = [_unpack_bf16_pair(w) for w in _load_token_tiles(x_ref, MOE_BLOCK, d // 2 // LANES)]
        x = jnp.concatenate([p[0] for p in pairs] + [p[1] for p in pairs], axis=1).astype(BF16)
        gu = _dot(x, wgu_scr[...]) + bgu_ref[0, 0]
        gate = jnp.minimum(gu[:, :f], SWIGLU_LIMIT)
        up = jnp.clip(gu[:, f:], -SWIGLU_LIMIT, SWIGLU_LIMIT)
        act = gate * _sigmoid(SWIGLU_ALPHA * gate) * (up + 1.0)
        y = _dot(act.astype(BF16), wd_scr[...]) + bd_ref[0, 0]
        _store_token_tiles(y_ref, _pack_bf16_pair(y[:, :d // 2], y[:, d // 2:]))

    @pl.when(i >= nu_ref[0])
    def _():
        y_ref[...] = jnp.zeros(y_ref.shape, y_ref.dtype)


def _expert_ffn(xs, block_e, n_used, layer, wgu, bgu, wd, bd):
    n_blocks = block_e.shape[0]
    depth, e, d, f2 = wgu.shape
    f = f2 // 2
    blk_rows = MOE_BLOCK * d // LANES
    later = jnp.where(block_e[None, :] > block_e[:, None], block_e[None, :], N_EXPERTS)
    nxt = jnp.min(later, axis=1)
    nxt = jnp.where(nxt < N_EXPERTS, nxt, -1).astype(jnp.int32)
    first = jnp.concatenate([jnp.ones((1,), jnp.int32),
                             (block_e[1:] != block_e[:-1]).astype(jnp.int32)])
    vis = (jnp.cumsum(first) - 1).astype(jnp.int32)
    spec = lambda shape, fn: pl.BlockSpec(shape, lambda i, be, nu, nx, vs: fn(i, be))
    grid_spec = pltpu.PrefetchScalarGridSpec(
        num_scalar_prefetch=4,
        grid=(n_blocks,),
        in_specs=[
            spec((blk_rows // 2, LANES), lambda i, be: (i, 0)),
            pl.BlockSpec(memory_space=pl.ANY),
            spec((1, 1, 1, f2), lambda i, be: (layer, be[i], 0, 0)),
            pl.BlockSpec(memory_space=pl.ANY),
            spec((1, 1, 1, d), lambda i, be: (layer, be[i], 0, 0)),
        ],
        out_specs=spec((blk_rows // 2, LANES), lambda i, be: (i, 0)),
        scratch_shapes=[pltpu.VMEM((2, d, f2), F32), pltpu.VMEM((2, f, d), F32),
                        pltpu.VMEM((d, f2), BF16), pltpu.VMEM((f, d), BF16),
                        pltpu.SemaphoreType.DMA((2, 2))],
    )
    return pl.pallas_call(
        functools.partial(_expert_kernel, layer=layer),
        out_shape=jax.ShapeDtypeStruct((n_blocks * blk_rows // 2, LANES), jnp.uint32),
        grid_spec=grid_spec,
        compiler_params=_params(("arbitrary",), 56),
        name="expert_ffn",
    )(block_e, n_used, nxt, vis, xs, wgu, bgu.reshape(depth, e, 1, f2), wd, bd.reshape(depth, e, 1, d))


def _combine_kernel(slot_ref, y_hbm, tw_ref, x_ref, g2_ref, gf_ref, o_ref, ybuf, sem, *, final):
    tm = x_ref.shape[0]

    half_tile = ROW_TILE // 2

    unroll = 4

    def issue(i, carry):
        for u in range(unroll):
            r = i * unroll + u
            dst = pl.ds(pl.multiple_of(r * half_tile, half_tile), half_tile)
            for k in range(TOP_K):
                s = lax.shift_right_logical(slot_ref[0, 0, r * TOP_K + k], 1)
                src = pl.ds(pl.multiple_of(s, half_tile), half_tile)
                pltpu.make_async_copy(y_hbm.at[src], ybuf.at[k, dst], sem).start(priority=k % 2)
        return carry

    lax.fori_loop(0, tm // unroll, issue, 0)
    for k in range(TOP_K):
        pltpu.make_async_copy(y_hbm.at[pl.ds(0, tm * half_tile)], ybuf.at[k], sem).wait()
    tw = tw_ref[...]
    lane = lax.broadcasted_iota(jnp.int32, tw.shape, 1)
    wks = [jnp.sum(jnp.where(lane == k, tw, 0.0), axis=-1, keepdims=True) for k in range(TOP_K)]
    tiles = [[_unpack_bf16_pair(w) for w in _load_token_tiles(ybuf.at[k], tm, half_tile)]
             for k in range(TOP_K)]
    pieces = []
    for part in range(2):
        for c in range(half_tile):
            moe = functools.reduce(lambda a, b: a + b,
                                   [wks[k] * tiles[k][c][part] for k in range(TOP_K)])
            col0 = (part * half_tile + c) * LANES
            cols = slice(col0, col0 + LANES)
            pieces.append(x_ref[:, cols] + g2_ref[0, :, cols] * moe)
    x = jnp.concatenate(pieces, axis=1)
    if final:
        ms = jnp.mean(x * x, axis=-1, keepdims=True)
        x = x * lax.rsqrt(ms + EPS) * gf_ref[...]
    o_ref[...] = x


def _combine(slots, y_slots, top_w, x_flat, g2, final_g, *, rows_per_mod, final, tm):
    t, d = x_flat.shape
    bpm = rows_per_mod // tm
    kern = functools.partial(_combine_kernel, final=final)
    return pl.pallas_call(
        kern,
        out_shape=jax.ShapeDtypeStruct((t, d), F32),
        grid=(t // tm,),
        in_specs=[
            pl.BlockSpec((1, 1, tm * TOP_K), lambda i: (i, 0, 0), memory_space=pltpu.SMEM),
            pl.BlockSpec(memory_space=pl.ANY),
            pl.BlockSpec((tm, LANES), lambda i: (i, 0)),
            pl.BlockSpec((tm, d), lambda i: (i, 0)),
            pl.BlockSpec((1, 1, d), lambda i: (i // bpm, 0, 0)),
            pl.BlockSpec((1, d), lambda i: (0, 0)),
        ],
        out_specs=pl.BlockSpec((tm, d), lambda i: (i, 0)),
        scratch_shapes=[pltpu.VMEM((TOP_K, tm * d // LANES // 2, LANES), jnp.uint32),
                        pltpu.SemaphoreType.DMA(())],
        compiler_params=_params(("arbitrary",), 32),
        name="moe_combine",
    )(slots.reshape(t // tm, 1, tm * TOP_K), y_slots, top_w, x_flat, g2, final_g.reshape(1, d))


def _route_plan(counts, n_tokens):
    counts = counts.astype(jnp.int32)
    nblk = (counts + MOE_BLOCK - 1) // MOE_BLOCK
    blk_end = jnp.cumsum(nblk)
    base = (blk_end - nblk) * MOE_BLOCK
    n_blocks = -(-(n_tokens * TOP_K) // MOE_BLOCK) + N_EXPERTS
    blocks = jnp.arange(n_blocks, dtype=jnp.int32)
    block_e = jnp.minimum(jnp.sum((blk_end[None, :] <= blocks[:, None]).astype(jnp.int32), axis=1),
                          N_EXPERTS - 1)
    return base, block_e, blk_end[-1:]


def _slots(top_i, base):
    experts = jnp.arange(N_EXPERTS, dtype=jnp.int32)
    e = top_i[:, :TOP_K]
    rank = top_i[:, TOP_K:2 * TOP_K]
    slot = jnp.sum(jnp.where(e[..., None] == experts, base, 0), axis=-1) + rank
    return (slot * ROW_TILE).reshape(-1)


def _moe(streams, counts, xs, layer, final_g, wgu, bgu, wd, bd, *, final):
    n_tokens = sum(st[3].shape[0] for st in streams)
    base, block_e, n_used = _route_plan(counts[0, :N_EXPERTS], n_tokens)
    slots = [_slots(st[1], base) for st in streams]
    for st, slot in zip(streams, slots):
        xs = _dispatch(slot, st[0], xs, tm=min(512, st[3].shape[0]))
    y_slots = _expert_ffn(xs, block_e, n_used, layer, wgu, bgu, wd, bd)
    outs = [_combine(slot, y_slots, st[2], st[3], st[4], final_g,
                     rows_per_mod=st[5], final=final, tm=256)
            for st, slot in zip(streams, slots)]
    return outs, xs


def _rope_tables(n):
    t = jnp.arange(n)
    row = (t // GRID_W).astype(F32)
    col = (t % GRID_W).astype(F32)
    quarter = DA_HEAD_DIM // 4
    freqs = ROPE_BASE ** (-jnp.arange(quarter, dtype=F32) / quarter)
    lane = jnp.arange(LANES)
    pos = jnp.where(((lane % DA_HEAD_DIM) // (DA_HEAD_DIM // 2))[None, :] == 0, row[:, None], col[:, None])
    ang = pos * freqs[lane % quarter][None, :]
    sign = jnp.where((lane % (DA_HEAD_DIM // 2)) < quarter, -1.0, 1.0)[None, :]
    return jnp.cos(ang), jnp.sin(ang) * sign


def _nbr_bias_tables(rpb):
    col = jnp.arange(GRID_W)
    col_start = jnp.clip(col - NA_KW // 2, 0, GRID_W - NA_KW)
    mask = (col[None, :] >= col_start[:, None]) & (col[None, :] < col_start[:, None] + NA_KW)
    dc = jnp.clip(col[None, :] - col[:, None], 1 - NA_KW, NA_KW - 1) + NA_KW - 1
    onehot = (dc[None] == jnp.arange(2 * NA_KW - 1)[:, None, None]).astype(F32)
    tab = jnp.einsum('lhdj,jqk->lhdqk', rpb.astype(F32), onehot, precision=lax.Precision.HIGHEST)
    tab = jnp.where(mask, tab, NEG_BIG)
    return jnp.concatenate([tab[:, :, :-1], tab[:, :, 1:]], axis=-1)


def _split_hi_lo(w):
    hi = w.astype(BF16)
    return hi, (w - hi.astype(F32)).astype(BF16)


def kernel(x, c, ctx, c_ctx, w_mod, b_mod, norm_mix_g, norm_ffn_g, w_in, da_lambda, da_subln_g,
           na_rpb, w_branch, w_out, w_router, b_router, w_gate_up, b_gate_up, w_down, b_down,
           final_g):
    b, s, d = x.shape
    assert d == ROW_TILE * LANES, "token rows must fill exactly one (8, 128) f32 tile"
    cl = ctx.shape[1]
    depth = w_mod.shape[0]
    tx, ty = b * s, b * cl

    n_rows = -(-(b + 1) // 8) * 8
    c_rows = jnp.concatenate([c, c_ctx[None], jnp.zeros((n_rows - b - 1, d), F32)], axis=0)
    mod = _modulation(c_rows, w_mod, b_mod)

    cos_t, sin_t = _rope_tables(s)
    tab_s = _dft_tables(s)
    tab_c = _dft_tables(cl)
    tab_g = _dft_tables(FT_GROUP_DIM)
    bias_tabs = _nbr_bias_tables(na_rpb)

    xf = x.reshape(tx, d)
    yf = ctx.reshape(ty, d)
    max_blocks = -(-((tx + ty) * TOP_K) // MOE_BLOCK) + N_EXPERTS
    xs_buf = jnp.zeros((max_blocks * MOE_BLOCK * ROW_TILE // 2, LANES), jnp.uint32)
    for l in range(depth):
        last = l == depth - 1
        lam_init = 0.8 - 0.6 * math.exp(-0.3 * l)
        mods = mod[l].reshape(n_rows, N_MOD, d)
        mx = [mods[:b, i].reshape(b, 1, d) for i in range(N_MOD)]
        my = [mods[b:b + 1, i].reshape(1, 1, d) for i in range(N_MOD)]
        w_in_l = w_in[l].astype(BF16)
        wb_l = w_branch[l].astype(BF16)
        wo_l = w_out[l].astype(BF16)
        wrh_l, wrl_l = _split_hi_lo(jnp.pad(w_router[l], ((0, 0), (0, LANES - N_EXPERTS))))
        br_l = jnp.pad(b_router[l], (0, LANES - N_EXPERTS), constant_values=NEG_BIG).reshape(1, LANES)
        zero_counts = jnp.zeros((8, LANES), F32)

        px = _in_proj(xf, norm_mix_g[l], mx[1], mx[0], w_in_l, cos_t, sin_t,
                      rows_per_mod=s, seq=s, rope=True, tm=min(2048, s))
        py = _in_proj(yf, norm_mix_g[l], my[1], my[0], w_in_l, cos_t, sin_t,
                      rows_per_mod=ty, seq=ty, rope=False, tm=min(1024, ty))
        px3 = px.reshape(b, s, IN_COLS)
        py3 = py.reshape(b, cl, IN_COLS)

        oa_x = _diff_attention(px3, [px3, py3], da_lambda[l], da_subln_g[l], lam_init,
                               tq=min(1024, s), n_split=4)
        ob_x = _nbr_attention(px3, py3, bias_tabs[l])
        oc_x = _fourier_mix(px3, tab_s, tab_g, tm=512)
        streams = []
        counts = zero_counts
        if not last:
            oa_y = _diff_attention(py3, [py3], da_lambda[l], da_subln_g[l], lam_init, tq=cl)
            ob_y = _ctx_attention(py3)
            oc_y = _fourier_mix(py3, tab_c, tab_g, tm=cl)
            yf, hy, tiy, twy, counts = _merge(
                oa_y.reshape(ty, -1), ob_y.reshape(ty, -1), oc_y.reshape(ty, -1), py, yf, my[2],
                wb_l, wo_l, norm_ffn_g[l], my[4], my[3], wrh_l, wrl_l, br_l, counts,
                rows_per_mod=ty, tm=256)
            streams.append((hy, tiy, twy, yf, my[5], ty))
        xf, hx, tix, twx, counts = _merge(
            oa_x.reshape(tx, -1), ob_x.reshape(tx, -1), oc_x.reshape(tx, -1), px, xf, mx[2],
            wb_l, wo_l, norm_ffn_g[l], mx[4], mx[3], wrh_l, wrl_l, br_l, counts,
            rows_per_mod=s, tm=512)
        streams.append((hx, tix, twx, xf, mx[5], s))

        outs, xs_buf = _moe(streams, counts, xs_buf, l, final_g, w_gate_up, b_gate_up, w_down, b_down,
                            final=last)
        xf = outs[-1]
        if not last:
            yf = outs[0]
    return xf.reshape(b, s, d)
```

```python
import functools
import math

import jax
import jax.numpy as jnp
from jax import lax
from jax.experimental import pallas as pl
from jax.experimental.pallas import tpu as pltpu

F32 = jnp.float32
BF16 = jnp.bfloat16

EPS = 1e-6
GRID_W = 64
N_MOD = 6
DA_HEADS = 4
DA_HEAD_DIM = 64
ROPE_BASE = 10000.0
NA_HEADS = 8
NA_HEAD_DIM = 64
NA_KH = 8
NA_KW = 16
FT_GROUPS = 4
FT_GROUP_DIM = 128
BRANCH_WIDTH = 512
N_BRANCHES = 3
N_EXPERTS = 32
TOP_K = 4
SWIGLU_LIMIT = 7.0
SWIGLU_ALPHA = 1.702

LANES = 128
MOE_BLOCK = 512
ROW_TILE = 8
NEG_BIG = -1e30
MIB = 1024 * 1024

COL_QA = 0
COL_KA = 512
COL_VA = 1024
COL_QN = 1536
COL_KN = 2048
COL_VN = 2560
COL_F = 3072
COL_GATE = 3584
IN_COLS = 6656


def _params(sem, vmem_mib):
    return pltpu.CompilerParams(dimension_semantics=sem, vmem_limit_bytes=vmem_mib * MIB)


def _sigmoid(v):
    return 0.5 * jnp.tanh(0.5 * v) + 0.5


def _dot(a, b):
    return jnp.dot(a, b, preferred_element_type=F32)


def _dot_nt(a, b):
    return lax.dot_general(a, b, (((1,), (1,)), ((), ())), preferred_element_type=F32)


def _mod_kernel(c_ref, w_ref, b_ref, o_ref):
    c = c_ref[...]
    s = c * _sigmoid(c)
    o_ref[0] = _dot(s.astype(BF16), w_ref[0].astype(BF16)) + b_ref[0]


def _modulation(c_rows, w_mod, b_mod):
    depth, d, n = w_mod.shape
    rows = c_rows.shape[0]
    tn = 1536
    return pl.pallas_call(
        _mod_kernel,
        out_shape=jax.ShapeDtypeStruct((depth, rows, n), F32),
        grid=(depth, n // tn),
        in_specs=[
            pl.BlockSpec((rows, d), lambda l, j: (0, 0)),
            pl.BlockSpec((1, d, tn), lambda l, j: (l, 0, j)),
            pl.BlockSpec((1, 1, tn), lambda l, j: (l, 0, j)),
        ],
        out_specs=pl.BlockSpec((1, rows, tn), lambda l, j: (l, 0, j)),
        compiler_params=_params(("arbitrary", "arbitrary"), 40),
        name="modulation",
    )(c_rows, w_mod, b_mod.reshape(depth, 1, n))


def _rope128(v, cos, sin_signed, first_half):
    partner = jnp.where(first_half, pltpu.roll(v, LANES - 16, axis=1), pltpu.roll(v, 16, axis=1))
    return v * cos + partner * sin_signed


def _in_proj_kernel(x_ref, g_ref, sc_ref, sh_ref, w_ref, cos_ref, sin_ref, o_ref, h_scr, *, rope, tn):
    j = pl.program_id(1)

    @pl.when(j == 0)
    def _():
        x = x_ref[...]
        ms = jnp.mean(x * x, axis=-1, keepdims=True)
        h = x * lax.rsqrt(ms + EPS) * g_ref[...]
        h = h * (1.0 + sc_ref[0]) + sh_ref[0]
        h_scr[...] = h.astype(BF16)

    acc = _dot(h_scr[...], w_ref[...])
    if rope:
        jq = COL_QA // tn
        jk = COL_KA // tn
        is_rot = jnp.logical_or(j == jq, j == jk)

        @pl.when(is_rot)
        def _():
            cos = cos_ref[...]
            sin = sin_ref[...]
            lane = lax.broadcasted_iota(jnp.int32, (1, LANES), 1)
            first_half = (lane % 32) < 16
            pieces = [
                _rope128(acc[:, k * LANES:(k + 1) * LANES], cos, sin, first_half)
                for k in range(tn // LANES)
            ]
            o_ref[...] = jnp.concatenate(pieces, axis=1).astype(BF16)

        @pl.when(jnp.logical_not(is_rot))
        def _():
            o_ref[...] = acc.astype(BF16)
    else:
        o_ref[...] = acc.astype(BF16)


def _in_proj(x_flat, g, sc, sh, w_bf16, cos_t, sin_t, *, rows_per_mod, seq, rope, tm):
    t, d = x_flat.shape
    n = w_bf16.shape[1]
    tn = 512
    bpm = rows_per_mod // tm
    bps = seq // tm
    kern = functools.partial(_in_proj_kernel, rope=rope, tn=tn)
    return pl.pallas_call(
        kern,
        out_shape=jax.ShapeDtypeStruct((t, n), BF16),
        grid=(t // tm, n // tn),
        in_specs=[
            pl.BlockSpec((tm, d), lambda i, j: (i, 0)),
            pl.BlockSpec((1, d), lambda i, j: (0, 0)),
            pl.BlockSpec((1, 1, d), lambda i, j: (i // bpm, 0, 0)),
            pl.BlockSpec((1, 1, d), lambda i, j: (i // bpm, 0, 0)),
            pl.BlockSpec((d, tn), lambda i, j: (0, j)),
            pl.BlockSpec((tm, LANES), lambda i, j: (i % bps, 0)),
            pl.BlockSpec((tm, LANES), lambda i, j: (i % bps, 0)),
        ],
        out_specs=pl.BlockSpec((tm, tn), lambda i, j: (i, j)),
        scratch_shapes=[pltpu.VMEM((tm, d), BF16)],
        compiler_params=_params(("arbitrary", "arbitrary"), 48),
        name="in_proj_rope" if rope else "in_proj",
    )(x_flat, g.reshape(1, d), sc, sh, w_bf16, cos_t, sin_t)


def _diff_attn_kernel(*refs, lam_init, n_kv, n_split):
    q_ref = refs[0]
    k_refs = refs[1:1 + n_kv]
    v_refs = refs[1 + n_kv:1 + 2 * n_kv]
    lam_ref, g_ref, o_ref, vext_scr = refs[1 + 2 * n_kv:]
    sizes = [v_ref.shape[1] for v_ref in v_refs]
    offs = [sum(sizes[:n]) for n in range(n_kv)]

    @pl.when(pl.program_id(2) == 0)
    def _():
        for v_ref, off, n in zip(v_refs, offs, sizes):
            vext_scr[off:off + n, :LANES] = v_ref[0]
            vext_scr[off:off + n, LANES:] = jnp.ones((n, LANES), BF16)

    lane = lax.broadcasted_iota(jnp.int32, (1, LANES), 1)
    masks = [jnp.where(lane < DA_HEAD_DIM, 1.0, 0.0).astype(BF16),
             jnp.where(lane >= DA_HEAD_DIM, 1.0, 0.0).astype(BF16)]
    ks = [k_ref[0] for k_ref in k_refs]
    lp = lam_ref[...]
    t1 = jnp.sum(lp[0:1] * lp[1:2], axis=-1, keepdims=True)
    t2 = jnp.sum(lp[2:3] * lp[3:4], axis=-1, keepdims=True)
    lam = jnp.exp(t1) - jnp.exp(t2) + lam_init
    tq = q_ref.shape[1]
    th = tq // n_split
    groups = [q_ref[0, g * th:(g + 1) * th, :] * (DA_HEAD_DIM ** -0.5) for g in range(n_split)]
    scores = [[[_dot_nt(q * m, k) for k in ks] for m in masks] for q in groups]
    maxes = [[functools.reduce(jnp.maximum, [jnp.max(p, axis=-1, keepdims=True) for p in parts])
              for parts in grp] for grp in scores]
    for g in range(n_split):
        outs = []
        for parts, mx in zip(scores[g], maxes[g]):
            oe = None
            for p, off, n in zip(parts, offs, sizes):
                term = _dot(jnp.exp(p - mx).astype(BF16), vext_scr[off:off + n, :])
                oe = term if oe is None else oe + term
            outs.append(oe[:, :LANES] * (1.0 / oe[:, LANES:LANES + 1]))
        o = outs[0] - lam * outs[1]
        ms = jnp.mean(o * o, axis=-1, keepdims=True)
        o = o * lax.rsqrt(ms + EPS) * g_ref[...] * (1.0 - lam_init)
        o_ref[0, g * th:(g + 1) * th, :] = o.astype(BF16)


def _diff_attention(p_q, kv_sources, lam_params, subln_g, lam_init, *, tq, n_split=1):
    b, nq, _ = p_q.shape
    qa0, ka0, va0 = COL_QA // LANES, COL_KA // LANES, COL_VA // LANES
    n_kv = len(kv_sources)
    in_specs = [pl.BlockSpec((1, tq, LANES), lambda bi, h, i: (bi, i, qa0 + h))]
    for src in kv_sources:
        in_specs.append(pl.BlockSpec((1, src.shape[1], LANES), lambda bi, h, i: (bi, 0, ka0 + h)))
    for src in kv_sources:
        in_specs.append(pl.BlockSpec((1, src.shape[1], LANES), lambda bi, h, i: (bi, 0, va0 + h)))
    in_specs.append(pl.BlockSpec((4, DA_HEAD_DIM), lambda bi, h, i: (0, 0)))
    in_specs.append(pl.BlockSpec((1, LANES), lambda bi, h, i: (0, 0)))
    kern = functools.partial(_diff_attn_kernel, lam_init=lam_init, n_kv=n_kv, n_split=n_split)
    return pl.pallas_call(
        kern,
        out_shape=jax.ShapeDtypeStruct((b, nq, DA_HEADS * LANES), BF16),
        grid=(b, DA_HEADS, nq // tq),
        in_specs=in_specs,
        out_specs=pl.BlockSpec((1, tq, LANES), lambda bi, h, i: (bi, i, h)),
        scratch_shapes=[pltpu.VMEM((sum(src.shape[1] for src in kv_sources), 2 * LANES), BF16)],
        compiler_params=_params(("arbitrary", "arbitrary", "arbitrary"), 48),
        name="diff_attention",
    )(p_q, *kv_sources, *kv_sources, lam_params, subln_g.reshape(1, LANES))


def _head_masks():
    lane = lax.broadcasted_iota(jnp.int32, (1, LANES), 1)
    return [jnp.where((lane // NA_HEAD_DIM) == hh, 1.0, 0.0).astype(BF16) for hh in range(2)]


def _softmax_stage(score_parts_per_head):
    maxes = [functools.reduce(jnp.maximum, [jnp.max(p, axis=-1, keepdims=True) for p in parts])
             for parts in score_parts_per_head]
    exps = [[jnp.exp(p - m) for p in parts] for parts, m in zip(score_parts_per_head, maxes)]
    invs = [1.0 / functools.reduce(lambda a, b: a + b, [jnp.sum(e, axis=-1, keepdims=True) for e in es])
            for es in exps]
    return exps, invs


def _head_pair_attention(q_pair, score_fn, value_fn):
    lane = lax.broadcasted_iota(jnp.int32, (1, LANES), 1)
    masks = _head_masks()
    exps, invs = _softmax_stage([score_fn(q_pair * masks[hh], hh) for hh in range(2)])
    outs = [value_fn([e.astype(BF16) for e in es]) * inv for es, inv in zip(exps, invs)]
    return jnp.where(lane < NA_HEAD_DIM, outs[0], outs[1])


def _nbr_attn_kernel(q_ref, kx_ref, vx_ref, kc_ref, vc_ref, bias_ref, o_ref, *, rows):
    r = pl.program_id(1)
    scale = NA_HEAD_DIM ** -0.5
    rs = jnp.clip(r - NA_KH // 2, 0, rows - NA_KH)
    dr0 = rs - r + NA_KH - 1
    win = NA_KH * GRID_W
    start = pl.multiple_of(rs * GRID_W, GRID_W)
    lane = lax.broadcasted_iota(jnp.int32, (1, LANES), 1)
    masks = _head_masks()
    n_pairs = NA_HEADS // 2
    cols = [slice(p * LANES, (p + 1) * LANES) for p in range(n_pairs)]
    nb = q_ref.shape[0]
    scores = []
    for bb in range(nb):
        for p in range(n_pairs):
            q_pair = q_ref[bb, :, cols[p]] * scale
            k_win = kx_ref[bb, pl.ds(start, win), cols[p]]
            k_ctx = kc_ref[bb, :, cols[p]]
            for hh in range(2):
                qm = q_pair * masks[hh]
                bias = jnp.concatenate(
                    [bias_ref[2 * p + hh, dr0 + 2 * jj] for jj in range(NA_KH // 2)], axis=1)
                scores.append([_dot_nt(qm, k_win) + bias, _dot_nt(qm, k_ctx)])
    exps, invs = _softmax_stage(scores)
    for bb in range(nb):
        for p in range(n_pairs):
            v_win = vx_ref[bb, pl.ds(start, win), cols[p]]
            v_ctx = vc_ref[bb, :, cols[p]]
            outs = []
            for hh in range(2):
                idx = (bb * n_pairs + p) * 2 + hh
                e_win, e_ctx = exps[idx]
                o = _dot(e_win.astype(BF16), v_win) + _dot(e_ctx.astype(BF16), v_ctx)
                outs.append(o * invs[idx])
            o_ref[bb, :, cols[p]] = jnp.where(lane < NA_HEAD_DIM, outs[0], outs[1]).astype(BF16)


def _nbr_attention(p_x, p_y, bias_t2):
    b, s, _ = p_x.shape
    c = p_y.shape[1]
    rows = s // GRID_W
    w = NA_HEADS * NA_HEAD_DIM
    jq, jk, jv = COL_QN // w, COL_KN // w, COL_VN // w
    kern = functools.partial(_nbr_attn_kernel, rows=rows)
    nb = 2 if b % 2 == 0 else 1
    return pl.pallas_call(
        kern,
        out_shape=jax.ShapeDtypeStruct((b, s, w), BF16),
        grid=(b // nb, rows),
        in_specs=[
            pl.BlockSpec((nb, GRID_W, w), lambda bi, r: (bi, r, jq)),
            pl.BlockSpec((nb, s, w), lambda bi, r: (bi, 0, jk)),
            pl.BlockSpec((nb, s, w), lambda bi, r: (bi, 0, jv)),
            pl.BlockSpec((nb, c, w), lambda bi, r: (bi, 0, jk)),
            pl.BlockSpec((nb, c, w), lambda bi, r: (bi, 0, jv)),
            pl.BlockSpec(bias_t2.shape, lambda bi, r: (0, 0, 0, 0)),
        ],
        out_specs=pl.BlockSpec((nb, GRID_W, w), lambda bi, r: (bi, r, 0)),
        compiler_params=_params(("arbitrary", "arbitrary"), 48),
        name="nbr_attention",
    )(p_x, p_x, p_x, p_y, p_y, bias_t2)


def _ctx_attn_kernel(q_ref, k_ref, v_ref, o_ref):
    scale = NA_HEAD_DIM ** -0.5
    for p in range(NA_HEADS // 2):
        cols = slice(p * LANES, (p + 1) * LANES)
        k = k_ref[0, :, cols]
        v = v_ref[0, :, cols]
        o = _head_pair_attention(
            q_ref[0, :, cols],
            lambda qm, hh, k=k: [_dot_nt(qm, k) * scale],
            lambda probs, v=v: _dot(probs[0], v))
        o_ref[0, :, cols] = o.astype(BF16)


def _ctx_attention(p_y):
    b, c, _ = p_y.shape
    w = NA_HEADS * NA_HEAD_DIM
    jq, jk, jv = COL_QN // w, COL_KN // w, COL_VN // w
    return pl.pallas_call(
        _ctx_attn_kernel,
        out_shape=jax.ShapeDtypeStruct((b, c, w), BF16),
        grid=(b,),
        in_specs=[
            pl.BlockSpec((1, c, w), lambda bi: (bi, 0, jq)),
            pl.BlockSpec((1, c, w), lambda bi: (bi, 0, jk)),
            pl.BlockSpec((1, c, w), lambda bi: (bi, 0, jv)),
        ],
        out_specs=pl.BlockSpec((1, c, w), lambda bi: (bi, 0, 0)),
        compiler_params=_params(("arbitrary",), 32),
        name="ctx_attention",
    )(p_y, p_y, p_y)


def _fourier_kernel(f_ref, cc_ref, sc_ref, cn_ref, sn_ref, o_ref, ab_scr, *, norm):
    i = pl.program_id(1)
    w = FT_GROUPS * FT_GROUP_DIM

    @pl.when(i == 0)
    def _():
        for g in range(FT_GROUPS):
            u = f_ref[0, :, g * FT_GROUP_DIM:(g + 1) * FT_GROUP_DIM]
            ab_scr[:, g * FT_GROUP_DIM:(g + 1) * FT_GROUP_DIM] = _dot(u, cc_ref[...]).astype(BF16)
            ab_scr[:, w + g * FT_GROUP_DIM:w + (g + 1) * FT_GROUP_DIM] = _dot(u, sc_ref[...]).astype(BF16)

    o = _dot(cn_ref[...], ab_scr[:, :w]) - _dot(sn_ref[...], ab_scr[:, w:])
    o_ref[0] = (o * norm).astype(BF16)


def _dft_tables(n):
    j = jnp.arange(n, dtype=jnp.int32)
    m = (j[:, None] * j[None, :]) % n
    ang = m.astype(F32) * (2.0 * math.pi / n)
    return jnp.cos(ang).astype(BF16), jnp.sin(ang).astype(BF16)


def _fourier_mix(p, tables_n, tables_c, *, tm):
    b, n, _ = p.shape
    w = FT_GROUPS * FT_GROUP_DIM
    cn, sn = tables_n
    cc, sc = tables_c
    jf = COL_F // w
    kern = functools.partial(_fourier_kernel, norm=1.0 / math.sqrt(n * FT_GROUP_DIM))
    return pl.pallas_call(
        kern,
        out_shape=jax.ShapeDtypeStruct((b, n, w), BF16),
        grid=(b, n // tm),
        in_specs=[
            pl.BlockSpec((1, n, w), lambda bi, i: (bi, 0, jf)),
            pl.BlockSpec(cc.shape, lambda bi, i: (0, 0)),
            pl.BlockSpec(sc.shape, lambda bi, i: (0, 0)),
            pl.BlockSpec((tm, n), lambda bi, i: (i, 0)),
            pl.BlockSpec((tm, n), lambda bi, i: (i, 0)),
        ],
        out_specs=pl.BlockSpec((1, tm, w), lambda bi, i: (bi, i, 0)),
        scratch_shapes=[pltpu.VMEM((n, 2 * w), BF16)],
        compiler_params=_params(("arbitrary", "arbitrary"), 48),
        name="fourier_mix",
    )(p, cc, sc, cn, sn)


def _store_token_tiles(ref, value):
    n, d = value.shape
    chunks = d // LANES
    for c in range(chunks):
        ref[pl.ds(c, n, stride=chunks), :] = value[:, c * LANES:(c + 1) * LANES]


def _load_token_tiles(ref, n, chunks):
    return [ref[pl.ds(c, n, stride=chunks), :] for c in range(chunks)]


def _pack_bf16_pair(lo, hi):
    ulo = lax.bitcast_convert_type(lo.astype(BF16).astype(F32), jnp.uint32)
    uhi = lax.bitcast_convert_type(hi.astype(BF16).astype(F32), jnp.uint32)
    return uhi | (ulo >> 16)


def _unpack_bf16_pair(w):
    lo = lax.bitcast_convert_type(w << 16, F32)
    hi = lax.bitcast_convert_type(w & jnp.uint32(0xFFFF0000), F32)
    return lo, hi


def _merge_kernel(oa_ref, ob_ref, oc_ref, *rest):
    n_gt = 2 * N_BRANCHES
    gt_refs = rest[:n_gt]
    (x_ref, g1_ref, wb_ref, wo_ref, gn_ref, sc_ref, sh_ref, wrh_ref, wrl_ref, br_ref, c0_ref,
     xo_ref, h_ref, ti_ref, tw_ref, cnt_ref) = rest[n_gt:]
    d = x_ref.shape[1]
    tm = x_ref.shape[0]
    half = d // 2

    @pl.when(pl.program_id(0) == 0)
    def _():
        cnt_ref[...] = c0_ref[...]
    m = [None, None]
    for i, o_ref in enumerate((oa_ref, ob_ref, oc_ref)):
        proj = _dot(o_ref[...], wb_ref[i])
        for hf in range(2):
            term = _sigmoid(gt_refs[2 * i + hf][...].astype(F32)) * proj[:, hf * half:(hf + 1) * half]
            m[hf] = term if m[hf] is None else m[hf] + term
    mix = _dot(jnp.concatenate(m, axis=1).astype(BF16), wo_ref[...])
    x = x_ref[...] + g1_ref[0] * mix
    xo_ref[...] = x
    ms = jnp.mean(x * x, axis=-1, keepdims=True)
    h = x * lax.rsqrt(ms + EPS) * gn_ref[...]
    h = h * (1.0 + sc_ref[0]) + sh_ref[0]
    _store_token_tiles(h_ref, _pack_bf16_pair(h[:, :half], h[:, half:]))
    h_hi = h.astype(BF16)
    h_lo = (h - h_hi.astype(F32)).astype(BF16)
    logits = (_dot(h_hi, wrh_ref[...]) + (_dot(h_hi, wrl_ref[...]) + _dot(h_lo, wrh_ref[...]))
              + br_ref[...])
    lane = lax.broadcasted_iota(jnp.int32, logits.shape, 1).astype(F32)
    cur = logits
    vals, idxs = [], []
    for _ in range(TOP_K):
        mx = jnp.max(cur, axis=-1, keepdims=True)
        ix = jnp.min(jnp.where(cur == mx, lane, float(LANES)), axis=-1, keepdims=True)
        vals.append(mx)
        idxs.append(ix)
        cur = jnp.where(lane == ix, -jnp.inf, cur)
    es = [jnp.exp(v - vals[0]) for v in vals]
    inv = 1.0 / functools.reduce(lambda a, b: a + b, es)
    member = functools.reduce(lambda a, b: a + b, [jnp.where(lane == ix, 1.0, 0.0) for ix in idxs])
    row_i = lax.broadcasted_iota(jnp.int32, (tm, tm), 0)
    col_i = lax.broadcasted_iota(jnp.int32, (tm, tm), 1)
    lower = jnp.where(col_i < row_i, 1.0, 0.0).astype(BF16)
    prefix = _dot(lower, member.astype(BF16)) + cnt_ref[0:1, :]
    ti = jnp.zeros(logits.shape, F32)
    tw = jnp.zeros(logits.shape, F32)
    for k in range(TOP_K):
        rank = jnp.sum(jnp.where(lane == idxs[k], prefix, 0.0), axis=-1, keepdims=True)
        ti = jnp.where(lane == float(k), idxs[k], ti)
        ti = jnp.where(lane == float(TOP_K + k), rank, ti)
        tw = jnp.where(lane == float(k), es[k] * inv, tw)
    ti_ref[...] = ti.astype(jnp.int32)
    tw_ref[...] = tw
    cnt_ref[...] = cnt_ref[...] + jnp.sum(member, axis=0, keepdims=True)


def _merge(oa, ob, oc, p_flat, x_flat, g1, w_branch, w_out, g_ffn, sc2, sh2, wr_hi, wr_lo, b_router,
           counts0, *, rows_per_mod, tm):
    t, d = x_flat.shape
    chunks = d // 2 // LANES
    bw = BRANCH_WIDTH
    bpm = rows_per_mod // tm
    row = lambda i: (i, 0)
    mod = lambda i: (i // bpm, 0, 0)
    whole2 = lambda i: (0, 0)
    n_gt = 2 * N_BRANCHES
    gw = N_BRANCHES * d // n_gt
    gate_specs = [pl.BlockSpec((tm, gw), functools.partial(lambda i, j: (i, j), j=COL_GATE // gw + j))
                  for j in range(n_gt)]
    return pl.pallas_call(
        _merge_kernel,
        out_shape=(
            jax.ShapeDtypeStruct((t, d), F32),
            jax.ShapeDtypeStruct((t * chunks, LANES), jnp.uint32),
            jax.ShapeDtypeStruct((t, LANES), jnp.int32),
            jax.ShapeDtypeStruct((t, LANES), F32),
            jax.ShapeDtypeStruct((8, LANES), F32),
        ),
        grid=(t // tm,),
        in_specs=[
            pl.BlockSpec((tm, bw), row),
            pl.BlockSpec((tm, bw), row),
            pl.BlockSpec((tm, bw), row),
            *gate_specs,
            pl.BlockSpec((tm, d), row),
            pl.BlockSpec((1, 1, d), mod),
            pl.BlockSpec((N_BRANCHES, bw, d), lambda i: (0, 0, 0)),
            pl.BlockSpec((d, d), whole2),
            pl.BlockSpec((1, d), whole2),
            pl.BlockSpec((1, 1, d), mod),
            pl.BlockSpec((1, 1, d), mod),
            pl.BlockSpec((d, LANES), whole2),
            pl.BlockSpec((d, LANES), whole2),
            pl.BlockSpec((1, LANES), whole2),
            pl.BlockSpec((8, LANES), whole2),
        ],
        out_specs=(
            pl.BlockSpec((tm, d), row),
            pl.BlockSpec((tm * chunks, LANES), row),
            pl.BlockSpec((tm, LANES), row),
            pl.BlockSpec((tm, LANES), row),
            pl.BlockSpec((8, LANES), whole2),
        ),
        compiler_params=_params(("arbitrary",), 48),
        name="merge_route",
    )(oa, ob, oc, *([p_flat] * n_gt), x_flat, g1, w_branch, w_out, g_ffn.reshape(1, d), sc2, sh2,
      wr_hi, wr_lo, b_router, counts0)


def _dispatch_kernel(slot_ref, h_ref, xs_in, xs_out, sem):
    del xs_in
    rows = h_ref.shape[0]
    half_tile = ROW_TILE // 2
    tm = rows // half_tile
    unroll = 4

    def issue(i, carry):
        for u in range(unroll):
            r = i * unroll + u
            src = h_ref.at[pl.ds(pl.multiple_of(r * half_tile, half_tile), half_tile)]
            for k in range(TOP_K):
                s = lax.shift_right_logical(slot_ref[0, 0, r * TOP_K + k], 1)
                dst = xs_out.at[pl.ds(pl.multiple_of(s, half_tile), half_tile)]
                pltpu.make_async_copy(src, dst, sem).start(priority=k % 2)
        return carry

    lax.fori_loop(0, tm // unroll, issue, 0)
    for k in range(TOP_K):
        pltpu.make_async_copy(h_ref, xs_out.at[pl.ds(0, rows)], sem).wait()


def _dispatch(row_starts, h_tiles, xs_init, *, tm):
    half_tile = ROW_TILE // 2
    t = h_tiles.shape[0] // half_tile
    return pl.pallas_call(
        _dispatch_kernel,
        out_shape=jax.ShapeDtypeStruct(xs_init.shape, xs_init.dtype),
        grid=(t // tm,),
        in_specs=[
            pl.BlockSpec((1, 1, tm * TOP_K), lambda i: (i, 0, 0), memory_space=pltpu.SMEM),
            pl.BlockSpec((tm * half_tile, LANES), lambda i: (i, 0)),
            pl.BlockSpec(memory_space=pl.ANY),
        ],
        out_specs=pl.BlockSpec(memory_space=pl.ANY),
        scratch_shapes=[pltpu.SemaphoreType.DMA(())],
        input_output_aliases={2: 0},
        compiler_params=_params(("arbitrary",), 32),
        name="moe_dispatch",
    )(row_starts.reshape(t // tm, 1, tm * TOP_K), h_tiles, xs_init)


def _expert_kernel(be_ref, nu_ref, nxt_ref, vis_ref, x_ref, wgu_hbm, bgu_ref, wd_hbm, bd_ref, y_ref,
                   wgu_f32, wd_f32, wgu_scr, wd_scr, sems, *, layer):
    i = pl.program_id(0)
    d, f2 = wgu_scr.shape
    f = f2 // 2
    new_expert = jnp.logical_or(i == 0, be_ref[i] != be_ref[jnp.maximum(i - 1, 0)])

    def weight_copies(e, slot):
        return (pltpu.make_async_copy(wgu_hbm.at[layer, e], wgu_f32.at[slot], sems.at[0, slot]),
                pltpu.make_async_copy(wd_hbm.at[layer, e], wd_f32.at[slot], sems.at[1, slot]))

    @pl.when(new_expert)
    def _():
        slot = vis_ref[i] % 2

        @pl.when(i == 0)
        def _():
            for cp in weight_copies(be_ref[0], 0):
                cp.start()

        for cp in weight_copies(be_ref[i], slot):
            cp.wait()

        @pl.when(nxt_ref[i] >= 0)
        def _():
            for cp in weight_copies(nxt_ref[i], 1 - slot):
                cp.start()

        chunk = 256
        for c in range(d // chunk):
            rows = slice(c * chunk, (c + 1) * chunk)
            wgu_scr[rows, :] = wgu_f32[slot, rows, :].astype(BF16)
        for c in range(f // chunk):
            rows = slice(c * chunk, (c + 1) * chunk)
            wd_scr[rows, :] = wd_f32[slot, rows, :].astype(BF16)

    @pl.when(i < nu_ref[0])
    def _():
        pairs = [_unpack_bf16_pair(w) for w in _load_token_tiles(x_ref, MOE_BLOCK, d // 2 // LANES)]
        x = jnp.concatenate([p[0] for p in pairs] + [p[1] for p in pairs], axis=1).astype(BF16)
        gu = _dot(x, wgu_scr[...]) + bgu_ref[0, 0]
        gate = jnp.minimum(gu[:, :f], SWIGLU_LIMIT)
        up = jnp.clip(gu[:, f:], -SWIGLU_LIMIT, SWIGLU_LIMIT)
        act = gate * _sigmoid(SWIGLU_ALPHA * gate) * (up + 1.0)
        y = _dot(act.astype(BF16), wd_scr[...]) + bd_ref[0, 0]
        _store_token_tiles(y_ref, _pack_bf16_pair(y[:, :d // 2], y[:, d // 2:]))

    @pl.when(i >= nu_ref[0])
    def _():
        y_ref[...] = jnp.zeros(y_ref.shape, y_ref.dtype)


def _expert_ffn(xs, block_e, n_used, layer, wgu, bgu, wd, bd):
    n_blocks = block_e.shape[0]
    depth, e, d, f2 = wgu.shape
    f = f2 // 2
    blk_rows = MOE_BLOCK * d // LANES
    later = jnp.where(block_e[None, :] > block_e[:, None], block_e[None, :], N_EXPERTS)
    nxt = jnp.min(later, axis=1)
    nxt = jnp.where(nxt < N_EXPERTS, nxt, -1).astype(jnp.int32)
    first = jnp.concatenate([jnp.ones((1,), jnp.int32),
                             (block_e[1:] != block_e[:-1]).astype(jnp.int32)])
    vis = (jnp.cumsum(first) - 1).astype(jnp.int32)
    spec = lambda shape, fn: pl.BlockSpec(shape, lambda i, be, nu, nx, vs: fn(i, be))
    grid_spec = pltpu.PrefetchScalarGridSpec(
        num_scalar_prefetch=4,
        grid=(n_blocks,),
        in_specs=[
            spec((blk_rows // 2, LANES), lambda i, be: (i, 0)),
            pl.BlockSpec(memory_space=pl.ANY),
            spec((1, 1, 1, f2), lambda i, be: (layer, be[i], 0, 0)),
            pl.BlockSpec(memory_space=pl.ANY),
            spec((1, 1, 1, d), lambda i, be: (layer, be[i], 0, 0)),
        ],
        out_specs=spec((blk_rows // 2, LANES), lambda i, be: (i, 0)),
        scratch_shapes=[pltpu.VMEM((2, d, f2), F32), pltpu.VMEM((2, f, d), F32),
                        pltpu.VMEM((d, f2), BF16), pltpu.VMEM((f, d), BF16),
                        pltpu.SemaphoreType.DMA((2, 2))],
    )
    return pl.pallas_call(
        functools.partial(_expert_kernel, layer=layer),
        out_shape=jax.ShapeDtypeStruct((n_blocks * blk_rows // 2, LANES), jnp.uint32),
        grid_spec=grid_spec,
        compiler_params=_params(("arbitrary",), 56),
        name="expert_ffn",
    )(block_e, n_used, nxt, vis, xs, wgu, bgu.reshape(depth, e, 1, f2), wd, bd.reshape(depth, e, 1, d))


def _combine_kernel(slot_ref, y_hbm, tw_ref, x_ref, g2_ref, gf_ref, o_ref, ybuf, sem, *, final):
    tm = x_ref.shape[0]

    half_tile = ROW_TILE // 2

    unroll = 4

    def issue(i, carry):
        for u in range(unroll):
            r = i * unroll + u
            dst = pl.ds(pl.multiple_of(r * half_tile, half_tile), half_tile)
            for k in range(TOP_K):
                s = lax.shift_right_logical(slot_ref[0, 0, r * TOP_K + k], 1)
                src = pl.ds(pl.multiple_of(s, half_tile), half_tile)
                pltpu.make_async_copy(y_hbm.at[src], ybuf.at[k, dst], sem).start(priority=k % 2)
        return carry

    lax.fori_loop(0, tm // unroll, issue, 0)
    for k in range(TOP_K):
        pltpu.make_async_copy(y_hbm.at[pl.ds(0, tm * half_tile)], ybuf.at[k], sem).wait()
    tw = tw_ref[...]
    lane = lax.broadcasted_iota(jnp.int32, tw.shape, 1)
    wks = [jnp.sum(jnp.where(lane == k, tw, 0.0), axis=-1, keepdims=True) for k in range(TOP_K)]
    tiles = [[_unpack_bf16_pair(w) for w in _load_token_tiles(ybuf.at[k], tm, half_tile)]
             for k in range(TOP_K)]
    pieces = []
    for part in range(2):
        for c in range(half_tile):
            moe = functools.reduce(lambda a, b: a + b,
                                   [wks[k] * tiles[k][c][part] for k in range(TOP_K)])
            col0 = (part * half_tile + c) * LANES
            cols = slice(col0, col0 + LANES)
            pieces.append(x_ref[:, cols] + g2_ref[0, :, cols] * moe)
    x = jnp.concatenate(pieces, axis=1)
    if final:
        ms = jnp.mean(x * x, axis=-1, keepdims=True)
        x = x * lax.rsqrt(ms + EPS) * gf_ref[...]
    o_ref[...] = x


def _combine(slots, y_slots, top_w, x_flat, g2, final_g, *, rows_per_mod, final, tm):
    t, d = x_flat.shape
    bpm = rows_per_mod // tm
    kern = functools.partial(_combine_kernel, final=final)
    return pl.pallas_call(
        kern,
        out_shape=jax.ShapeDtypeStruct((t, d), F32),
        grid=(t // tm,),
        in_specs=[
            pl.BlockSpec((1, 1, tm * TOP_K), lambda i: (i, 0, 0), memory_space=pltpu.SMEM),
            pl.BlockSpec(memory_space=pl.ANY),
            pl.BlockSpec((tm, LANES), lambda i: (i, 0)),
            pl.BlockSpec((tm, d), lambda i: (i, 0)),
            pl.BlockSpec((1, 1, d), lambda i: (i // bpm, 0, 0)),
            pl.BlockSpec((1, d), lambda i: (0, 0)),
        ],
        out_specs=pl.BlockSpec((tm, d), lambda i: (i, 0)),
        scratch_shapes=[pltpu.VMEM((TOP_K, tm * d // LANES // 2, LANES), jnp.uint32),
                        pltpu.SemaphoreType.DMA(())],
        compiler_params=_params(("arbitrary",), 32),
        name="moe_combine",
    )(slots.reshape(t // tm, 1, tm * TOP_K), y_slots, top_w, x_flat, g2, final_g.reshape(1, d))


def _route_plan(counts, n_tokens):
    counts = counts.astype(jnp.int32)
    nblk = (counts + MOE_BLOCK - 1) // MOE_BLOCK
    blk_end = jnp.cumsum(nblk)
    base = (blk_end - nblk) * MOE_BLOCK
    n_blocks = -(-(n_tokens * TOP_K) // MOE_BLOCK) + N_EXPERTS
    blocks = jnp.arange(n_blocks, dtype=jnp.int32)
    block_e = jnp.minimum(jnp.sum((blk_end[None, :] <= blocks[:, None]).astype(jnp.int32), axis=1),
                          N_EXPERTS - 1)
    return base, block_e, blk_end[-1:]


def _slots(top_i, base):
    experts = jnp.arange(N_EXPERTS, dtype=jnp.int32)
    e = top_i[:, :TOP_K]
    rank = top_i[:, TOP_K:2 * TOP_K]
    slot = jnp.sum(jnp.where(e[..., None] == experts, base, 0), axis=-1) + rank
    return (slot * ROW_TILE).reshape(-1)


def _moe(streams, counts, xs, layer, final_g, wgu, bgu, wd, bd, *, final):
    n_tokens = sum(st[3].shape[0] for st in streams)
    base, block_e, n_used = _route_plan(counts[0, :N_EXPERTS], n_tokens)
    slots = [_slots(st[1], base) for st in streams]
    for st, slot in zip(streams, slots):
        xs = _dispatch(slot, st[0], xs, tm=min(1024, st[3].shape[0]))
    y_slots = _expert_ffn(xs, block_e, n_used, layer, wgu, bgu, wd, bd)
    outs = [_combine(slot, y_slots, st[2], st[3], st[4], final_g,
                     rows_per_mod=st[5], final=final, tm=min(512, st[3].shape[0]))
            for st, slot in zip(streams, slots)]
    return outs, xs


def _rope_tables(n):
    t = jnp.arange(n)
    row = (t // GRID_W).astype(F32)
    col = (t % GRID_W).astype(F32)
    quarter = DA_HEAD_DIM // 4
    freqs = ROPE_BASE ** (-jnp.arange(quarter, dtype=F32) / quarter)
    lane = jnp.arange(LANES)
    pos = jnp.where(((lane % DA_HEAD_DIM) // (DA_HEAD_DIM // 2))[None, :] == 0, row[:, None], col[:, None])
    ang = pos * freqs[lane % quarter][None, :]
    sign = jnp.where((lane % (DA_HEAD_DIM // 2)) < quarter, -1.0, 1.0)[None, :]
    return jnp.cos(ang), jnp.sin(ang) * sign


def _nbr_bias_tables(rpb):
    col = jnp.arange(GRID_W)
    col_start = jnp.clip(col - NA_KW // 2, 0, GRID_W - NA_KW)
    mask = (col[None, :] >= col_start[:, None]) & (col[None, :] < col_start[:, None] + NA_KW)
    dc = jnp.clip(col[None, :] - col[:, None], 1 - NA_KW, NA_KW - 1) + NA_KW - 1
    onehot = (dc[None] == jnp.arange(2 * NA_KW - 1)[:, None, None]).astype(F32)
    tab = jnp.einsum('lhdj,jqk->lhdqk', rpb.astype(F32), onehot, precision=lax.Precision.HIGHEST)
    tab = jnp.where(mask, tab, NEG_BIG)
    return jnp.concatenate([tab[:, :, :-1], tab[:, :, 1:]], axis=-1)


def _split_hi_lo(w):
    hi = w.astype(BF16)
    return hi, (w - hi.astype(F32)).astype(BF16)


def kernel(x, c, ctx, c_ctx, w_mod, b_mod, norm_mix_g, norm_ffn_g, w_in, da_lambda, da_subln_g,
           na_rpb, w_branch, w_out, w_router, b_router, w_gate_up, b_gate_up, w_down, b_down,
           final_g):
    b, s, d = x.shape
    assert d == ROW_TILE * LANES, "token rows must fill exactly one (8, 128) f32 tile"
    cl = ctx.shape[1]
    depth = w_mod.shape[0]
    tx, ty = b * s, b * cl

    n_rows = -(-(b + 1) // 8) * 8
    c_rows = jnp.concatenate([c, c_ctx[None], jnp.zeros((n_rows - b - 1, d), F32)], axis=0)
    mod = _modulation(c_rows, w_mod, b_mod)

    cos_t, sin_t = _rope_tables(s)
    tab_s = _dft_tables(s)
    tab_c = _dft_tables(cl)
    tab_g = _dft_tables(FT_GROUP_DIM)
    bias_tabs = _nbr_bias_tables(na_rpb)

    xf = x.reshape(tx, d)
    yf = ctx.reshape(ty, d)
    max_blocks = -(-((tx + ty) * TOP_K) // MOE_BLOCK) + N_EXPERTS
    xs_buf = jnp.zeros((max_blocks * MOE_BLOCK * ROW_TILE // 2, LANES), jnp.uint32)
    for l in range(depth):
        last = l == depth - 1
        lam_init = 0.8 - 0.6 * math.exp(-0.3 * l)
        mods = mod[l].reshape(n_rows, N_MOD, d)
        mx = [mods[:b, i].reshape(b, 1, d) for i in range(N_MOD)]
        my = [mods[b:b + 1, i].reshape(1, 1, d) for i in range(N_MOD)]
        w_in_l = w_in[l].astype(BF16)
        wb_l = w_branch[l].astype(BF16)
        wo_l = w_out[l].astype(BF16)
        wrh_l, wrl_l = _split_hi_lo(jnp.pad(w_router[l], ((0, 0), (0, LANES - N_EXPERTS))))
        br_l = jnp.pad(b_router[l], (0, LANES - N_EXPERTS), constant_values=NEG_BIG).reshape(1, LANES)
        zero_counts = jnp.zeros((8, LANES), F32)

        px = _in_proj(xf, norm_mix_g[l], mx[1], mx[0], w_in_l, cos_t, sin_t,
                      rows_per_mod=s, seq=s, rope=True, tm=min(2048, s))
        py = _in_proj(yf, norm_mix_g[l], my[1], my[0], w_in_l, cos_t, sin_t,
                      rows_per_mod=ty, seq=ty, rope=False, tm=min(1024, ty))
        px3 = px.reshape(b, s, IN_COLS)
        py3 = py.reshape(b, cl, IN_COLS)

        oa_x = _diff_attention(px3, [px3, py3], da_lambda[l], da_subln_g[l], lam_init,
                               tq=min(1024, s), n_split=4)
        ob_x = _nbr_attention(px3, py3, bias_tabs[l])
        oc_x = _fourier_mix(px3, tab_s, tab_g, tm=512)
        streams = []
        counts = zero_counts
        if not last:
            oa_y = _diff_attention(py3, [py3], da_lambda[l], da_subln_g[l], lam_init, tq=cl)
            ob_y = _ctx_attention(py3)
            oc_y = _fourier_mix(py3, tab_c, tab_g, tm=cl)
            yf, hy, tiy, twy, counts = _merge(
                oa_y.reshape(ty, -1), ob_y.reshape(ty, -1), oc_y.reshape(ty, -1), py, yf, my[2],
                wb_l, wo_l, norm_ffn_g[l], my[4], my[3], wrh_l, wrl_l, br_l, counts,
                rows_per_mod=ty, tm=256)
            streams.append((hy, tiy, twy, yf, my[5], ty))
        xf, hx, tix, twx, counts = _merge(
            oa_x.reshape(tx, -1), ob_x.reshape(tx, -1), oc_x.reshape(tx, -1), px, xf, mx[2],
            wb_l, wo_l, norm_ffn_g[l], mx[4], mx[3], wrh_l, wrl_l, br_l, counts,
            rows_per_mod=s, tm=512)
        streams.append((hx, tix, twx, xf, mx[5], s))

        outs, xs_buf = _moe(streams, counts, xs_buf, l, final_g, w_gate_up, b_gate_up, w_down, b_down,
                            final=last)
        xf = outs[-1]
        if not last:
            yf = outs[0]
    return xf.reshape(b, s, d)
```
